```python
import math
import jax, jax.numpy as jnp
from jax import lax
import numpy as np

D_MODEL = 1024
BATCH = 8
SEQ = 2048
DEPTH = 1
DEC_BATCH = 2
DEC_SEQ = 8192
PAST_LEN = 128

GRID_W = 64
MIX_WIDTH = D_MODEL
ATT_WIDTH = MIX_WIDTH // 2
HEAD_DIM = 64
N_HEADS = ATT_WIDTH // HEAD_DIM
POOL_WIDTH = MIX_WIDTH - ATT_WIDTH
POOL_WINDOWS = (2, 4, 8, 16)
N_POOL = len(POOL_WINDOWS)
POOL_GROUP_DIM = POOL_WIDTH // N_POOL
IN_WIDTH = 3 * ATT_WIDTH + POOL_WIDTH
WIN_ROWS = 8
WIN_COLS = 16
D_FF = 2816
CONV_W = 3
EPS = 1e-6

kernel_name = "hybrid_natten_pool_encoder"


def rmsnorm(x, g):
    xf = x.astype(jnp.float32)
    y = xf * lax.rsqrt(jnp.mean(xf * xf, axis=-1, keepdims=True) + EPS)
    return (y * g.astype(jnp.float32)).astype(x.dtype)


def neighborhood_attention(q, k, v, rpb):
    B, T, H, hd = q.shape
    rows = T // GRID_W
    kh = min(WIN_ROWS, rows)
    kw = WIN_COLS
    qg = q.reshape(B, rows, GRID_W, H, hd)
    kg = k.reshape(B, rows, GRID_W, H, hd)
    vg = v.reshape(B, rows, GRID_W, H, hd)
    cols = jnp.arange(GRID_W)
    col_start = jnp.clip(cols - kw // 2, 0, GRID_W - kw)
    col_idx = col_start[:, None] + jnp.arange(kw)[None, :]
    dc = col_idx - cols[:, None] + (WIN_COLS - 1)
    scale = HEAD_DIM ** -0.5

    def row_block(r):
        rs = jnp.clip(r - kh // 2, 0, rows - kh)
        q_r = lax.dynamic_index_in_dim(qg, r, axis=1, keepdims=False)
        k_rows = lax.dynamic_slice_in_dim(kg, rs, kh, axis=1)
        v_rows = lax.dynamic_slice_in_dim(vg, rs, kh, axis=1)
        k_nb = k_rows[:, :, col_idx]
        v_nb = v_rows[:, :, col_idx]
        s = jnp.einsum('bqhd,biqjhd->bhqij', q_r, k_nb)
        dr = rs + jnp.arange(kh) - r + (WIN_ROWS - 1)
        bias = rpb[:, dr[None, :, None], dc[:, None, :]]
        s = s.astype(jnp.float32) * scale + bias.astype(jnp.float32)[None]
        p = jax.nn.softmax(s.reshape(B, H, GRID_W, kh * kw), axis=-1)
        p = p.reshape(B, H, GRID_W, kh, kw).astype(v.dtype)
        return jnp.einsum('bhqij,biqjhd->bqhd', p, v_nb)

    out = lax.map(row_block, jnp.arange(rows))
    return jnp.transpose(out, (1, 0, 2, 3, 4)).reshape(B, T, H * hd)


def multiscale_pool(u, w_pool, pool_scale):
    B, T, _ = u.shape
    ug = u.reshape(B, T, N_POOL, POOL_GROUP_DIM).astype(jnp.float32)
    cs = jnp.concatenate([jnp.zeros((B, 1, N_POOL, POOL_GROUP_DIM), jnp.float32),
                          jnp.cumsum(ug, axis=1)], axis=1)
    t = jnp.arange(T)
    means = []
    for g, w in enumerate(POOL_WINDOWS):
        lo = jnp.clip(t - w // 2, 0, T)
        hi = jnp.clip(t - w // 2 + w, 0, T)
        cnt = (hi - lo).astype(jnp.float32)
        means.append((cs[:, hi, g] - cs[:, lo, g]) / cnt[None, :, None])
    mixed = (jnp.stack(means, axis=2) - ug).astype(u.dtype)
    y = jnp.einsum('btgc,gcd->btgd', mixed, w_pool)
    return y.reshape(B, T, POOL_WIDTH) * pool_scale


def dwconv3(x, w, b):
    xp = jnp.pad(x, ((0, 0), (1, 1), (0, 0)))
    return xp[:, :-2] * w[0] + xp[:, 1:-1] * w[1] + xp[:, 2:] * w[2] + b


def encoder_layer(x, c, w_ada, b_ada, norm1_g, norm2_g, w_in, q_norm_g, k_norm_g,
                  rpb, w_pool, pool_scale, w_out, w_up, conv_w, conv_b, w_down):
    B, T, D = x.shape
    mod = jnp.einsum('bd,de->be', jax.nn.silu(c), w_ada) + b_ada
    sh1, sc1, g1, sh2, sc2, g2 = jnp.split(mod[:, None, :], 6, axis=-1)

    h = rmsnorm(x, norm1_g) * (1 + sc1) + sh1
    proj = jnp.einsum('btd,de->bte', h, w_in)
    q, k, v, u = jnp.split(proj, [ATT_WIDTH, 2 * ATT_WIDTH, 3 * ATT_WIDTH], axis=-1)
    q = rmsnorm(q.reshape(B, T, N_HEADS, HEAD_DIM), q_norm_g)
    k = rmsnorm(k.reshape(B, T, N_HEADS, HEAD_DIM), k_norm_g)
    v = v.reshape(B, T, N_HEADS, HEAD_DIM)
    a = neighborhood_attention(q, k, v, rpb)
    p = multiscale_pool(u, w_pool, pool_scale)
    mixed = jnp.einsum('bte,ed->btd', jnp.concatenate([a, p], axis=-1), w_out)
    x = x + g1 * mixed

    h = rmsnorm(x, norm2_g) * (1 + sc2) + sh2
    up = dwconv3(jnp.einsum('btd,df->btf', h, w_up), conv_w, conv_b)
    gate, val = jnp.split(up, 2, axis=-1)
    y = jnp.einsum('btf,fd->btd', jax.nn.silu(gate) * val, w_down)
    return x + g2 * y


def setup_inputs(seed: int = 0) -> dict:
    key = jax.random.key(seed)
    ks = jax.random.split(key, 24)
    nrm = jax.random.normal
    L, D, F = DEPTH, D_MODEL, D_FF
    f32 = jnp.float32
    conv_center = jnp.zeros((CONV_W, 1), f32).at[CONV_W // 2].set(1.0)
    return {
        "x_prompt": nrm(ks[0], (BATCH, SEQ, D), f32),
        "x_sample": nrm(ks[1], (DEC_BATCH, DEC_SEQ, D), f32),
        "c_prompt": nrm(ks[2], (BATCH, D), f32),
        "c_sample": nrm(ks[3], (DEC_BATCH, D), f32),
        "w_ada": nrm(ks[4], (L, D, 6 * D), f32) * (0.5 * D ** -0.5),
        "b_ada": nrm(ks[5], (L, 6 * D), f32) * 0.02,
        "norm1_g": 1.0 + 0.05 * nrm(ks[6], (L, D), f32),
        "norm2_g": 1.0 + 0.05 * nrm(ks[7], (L, D), f32),
        "w_in": nrm(ks[8], (L, D, IN_WIDTH), f32) * D ** -0.5,
        "q_norm_g": 1.0 + 0.05 * nrm(ks[9], (L, HEAD_DIM), f32),
        "k_norm_g": 1.0 + 0.05 * nrm(ks[10], (L, HEAD_DIM), f32),
        "rpb": 0.5 * nrm(ks[11], (L, N_HEADS, 2 * WIN_ROWS - 1, 2 * WIN_COLS - 1), f32),
        "w_pool": nrm(ks[12], (L, N_POOL, POOL_GROUP_DIM, POOL_GROUP_DIM), f32) * POOL_GROUP_DIM ** -0.5,
        "pool_scale": 1.0 + 0.1 * nrm(ks[13], (L, POOL_WIDTH), f32),
        "w_out": nrm(ks[14], (L, MIX_WIDTH, D), f32) * MIX_WIDTH ** -0.5,
        "w_up": nrm(ks[15], (L, D, 2 * F), f32) * D ** -0.5,
        "conv_w": conv_center[None] + 0.2 * nrm(ks[16], (L, CONV_W, 2 * F), f32),
        "conv_b": 0.02 * nrm(ks[17], (L, 2 * F), f32),
        "w_down": nrm(ks[18], (L, F, D), f32) * F ** -0.5,
    }


def reference(x_prompt, x_sample, c_prompt, c_sample, w_ada, b_ada, norm1_g, norm2_g,
              w_in, q_norm_g, k_norm_g, rpb, w_pool, pool_scale, w_out, w_up,
              conv_w, conv_b, w_down):
    y_prompt = x_prompt
    y_sample = x_sample
    for l in range(DEPTH):
        params = (w_ada[l], b_ada[l], norm1_g[l], norm2_g[l], w_in[l], q_norm_g[l],
                  k_norm_g[l], rpb[l], w_pool[l], pool_scale[l], w_out[l], w_up[l],
                  conv_w[l], conv_b[l], w_down[l])
        y_prompt = encoder_layer(y_prompt, c_prompt, *params)
        y_sample = encoder_layer(y_sample, c_sample, *params)
    return (y_prompt, y_sample)
```

```python
import functools

import numpy as np
import jax
import jax.numpy as jnp
from jax import lax
from jax.experimental import pallas as pl
from jax.experimental.pallas import tpu as pltpu

F32 = jnp.float32
BF16 = jnp.bfloat16

D_MODEL = 1024
GRID_W = 64
HEAD_DIM = 64
N_HEADS = 8
N_PAIRS = N_HEADS // 2
ATT_WIDTH = N_HEADS * HEAD_DIM
POOL_WINDOWS = (2, 4, 8, 16)
POOL_GROUP = 128
POOL_WIDTH = POOL_GROUP * len(POOL_WINDOWS)
WIN_ROWS = 8
WIN_COLS = 16
D_FF = 2816
EPS = 1e-6
NEG = -1e30

LANES = 128
MXU_N = 256
TM = 512
ROWS_PER_TILE = TM // GRID_W
HALO_ROWS = 4
HALO_TOK = HALO_ROWS * GRID_W
SUB = 8
SUB16 = 16
FC = 256
VMEM_LIMIT = 56 * 1024 * 1024


def _const_spec(shape):
    nd = len(shape)
    return pl.BlockSpec(shape, lambda i: (0,) * nd, pipeline_mode=pl.Buffered(1))


def _params(vmem=VMEM_LIMIT):
    return pltpu.CompilerParams(dimension_semantics=("arbitrary",), vmem_limit_bytes=vmem)


def _ada_kernel(c_ref, w_ref, b_ref, o_ref):
    c = c_ref[...]
    s = c * jax.nn.sigmoid(c)
    o_ref[...] = jnp.dot(s, w_ref[...], preferred_element_type=F32,
                         precision=lax.Precision.HIGHEST) + b_ref[...]


def _modulation(c, w_ada, b_ada):
    rows = c.shape[0]
    n = w_ada.shape[1]
    bn = 1536
    return pl.pallas_call(
        _ada_kernel,
        grid=(n // bn,),
        in_specs=[pl.BlockSpec((rows, D_MODEL), lambda j: (0, 0)),
                  pl.BlockSpec((D_MODEL, bn), lambda j: (0, j)),
                  pl.BlockSpec((1, bn), lambda j: (0, j))],
        out_specs=pl.BlockSpec((rows, bn), lambda j: (0, j)),
        out_shape=jax.ShapeDtypeStruct((rows, n), F32),
        compiler_params=_params(),
        name="modulation",
    )(c, w_ada, b_ada.reshape(1, n))


def _norm_mod(x, g, scale, shift):
    ms = jnp.mean(x * x, axis=-1, keepdims=True)
    return ((x * lax.rsqrt(ms + EPS)) * g) * (1.0 + scale) + shift


def _inproj_kernel(x_ref, mod_ref, g_ref, w_ref, seg_ref, qg_ref, kg_ref,
                   qx_ref, k_ref, v_ref, u_ref):
    mod = mod_ref[0]
    h = _norm_mod(x_ref[...], g_ref[...], mod[1:2], mod[0:1]).astype(BF16)
    lane = lax.broadcasted_iota(jnp.int32, (TM, LANES), 1)
    first_head = lane < HEAD_DIM

    def proj(c):
        return jnp.dot(h, w_ref[:, c * MXU_N:(c + 1) * MXU_N], preferred_element_type=F32)

    def head_norm(y, gain):
        ms = jnp.dot((y * y).astype(BF16), seg_ref[...], preferred_element_type=F32)
        return (y * lax.rsqrt(ms + EPS)) * gain

    n_chunks = ATT_WIDTH // MXU_N
    for c in range(n_chunks):
        qn = head_norm(proj(c), qg_ref[:, c * MXU_N:(c + 1) * MXU_N])
        for pp in range(MXU_N // LANES):
            pair = qn[:, pp * LANES:(pp + 1) * LANES]
            base = (c * (MXU_N // LANES) + pp) * 2 * LANES
            qx_ref[:, base:base + LANES] = jnp.where(first_head, pair, 0.0).astype(BF16)
            qx_ref[:, base + LANES:base + 2 * LANES] = jnp.where(first_head, 0.0, pair).astype(BF16)
    for c in range(n_chunks):
        kn = head_norm(proj(n_chunks + c), kg_ref[:, c * MXU_N:(c + 1) * MXU_N])
        k_ref[:, c * MXU_N:(c + 1) * MXU_N] = kn.astype(BF16)
    for c in range(n_chunks):
        v_ref[:, c * MXU_N:(c + 1) * MXU_N] = proj(2 * n_chunks + c).astype(BF16)
    for c in range(POOL_WIDTH // MXU_N):
        u_ref[:, c * MXU_N:(c + 1) * MXU_N] = proj(3 * n_chunks + c).astype(BF16)


def _inproj(x2, mod3, boff, seq, g1, w_in, seg, qg, kg):
    n = x2.shape[0]
    tps = seq // TM
    out_w = (2 * ATT_WIDTH, ATT_WIDTH, ATT_WIDTH, POOL_WIDTH)
    return pl.pallas_call(
        _inproj_kernel,
        grid=(n // TM,),
        in_specs=[pl.BlockSpec((TM, D_MODEL), lambda i: (i, 0)),
                  pl.BlockSpec((1, 6, D_MODEL), lambda i: (boff + i // tps, 0, 0)),
                  _const_spec((1, D_MODEL)),
                  _const_spec(w_in.shape),
                  _const_spec(seg.shape),
                  _const_spec((1, ATT_WIDTH)),
                  _const_spec((1, ATT_WIDTH))],
        out_specs=[pl.BlockSpec((TM, w), lambda i: (i, 0)) for w in out_w],
        out_shape=[jax.ShapeDtypeStruct((n, w), BF16) for w in out_w],
        compiler_params=_params(),
        name="inproj",
    )(x2, mod3, g1, w_in, seg, qg, kg)


def _mix_kernel(q_ref, kp_ref, kc_ref, kn_ref, vp_ref, vc_ref, vn_ref,
                up_ref, uc_ref, un_ref, x_ref, mod_ref, bias_ref,
                wpool_ref, pscale_ref, wout_ref, o_ref,
                kbuf, vbuf, mixed, *, tps, rows, seq):
    ti = lax.rem(pl.program_id(0), tps)
    r0 = ti * ROWS_PER_TILE

    kbuf[0:HALO_TOK, :] = kp_ref[...]
    kbuf[HALO_TOK:HALO_TOK + TM, :] = kc_ref[...]
    kbuf[HALO_TOK + TM:, :] = kn_ref[...]
    vbuf[0:HALO_TOK, :] = vp_ref[...]
    vbuf[HALO_TOK:HALO_TOK + TM, :] = vc_ref[...]
    vbuf[HALO_TOK + TM:, :] = vn_ref[...]

    lane = lax.broadcasted_iota(jnp.int32, (GRID_W, LANES), 1)
    n_keys = WIN_ROWS * GRID_W

    def row_body(lr, carry):
        r = r0 + lr
        rs = jnp.clip(r - WIN_ROWS // 2, 0, rows - WIN_ROWS)
        var = r - rs
        off = pl.multiple_of((rs - r0 + HALO_ROWS) * GRID_W, GRID_W)
        qoff = pl.multiple_of(lr * GRID_W, GRID_W)
        for p in range(N_PAIRS):
            qp = q_ref[pl.ds(qoff, GRID_W), p * 2 * LANES:(p + 1) * 2 * LANES]
            q2 = jnp.concatenate([qp[:, :LANES], qp[:, LANES:]], axis=0)
            kw = kbuf[pl.ds(off, n_keys), p * LANES:(p + 1) * LANES]
            vw = vbuf[pl.ds(off, n_keys), p * LANES:(p + 1) * LANES]
            s = lax.dot_general(q2, kw, (((1,), (1,)), ((), ())),
                                preferred_element_type=F32)
            s = s + bias_ref[var, p]
            m = jnp.max(s, axis=-1, keepdims=True)
            e = jnp.exp(s - m)
            l = jnp.sum(e, axis=-1, keepdims=True)
            o2 = jnp.dot(e.astype(BF16), vw, preferred_element_type=F32) / l
            o = jnp.where(lane < HEAD_DIM, o2[:GRID_W], o2[GRID_W:])
            mixed[pl.ds(qoff, GRID_W), p * LANES:(p + 1) * LANES] = o.astype(BF16)
        return carry

    lax.fori_loop(0, ROWS_PER_TILE, row_body, 0)

    uprev = jnp.where(ti > 0, up_ref[...].astype(F32)[SUB:], 0.0)
    unext = jnp.where(ti < tps - 1, un_ref[...].astype(F32)[:SUB], 0.0)
    ue = jnp.concatenate([uprev, uc_ref[...].astype(F32), unext], axis=0)
    ext = TM + 2 * SUB
    tpos = ti * TM + lax.broadcasted_iota(jnp.int32, (TM, POOL_GROUP), 0)
    for g, w in enumerate(POOL_WINDOWS):
        xg = ue[:, g * POOL_GROUP:(g + 1) * POOL_GROUP]
        acc = xg + pltpu.roll(xg, 1, 0)
        half = 1
        while 2 * half < w:
            acc = pltpu.roll(acc, half, 0) + pltpu.roll(acc, ext - half, 0)
            half *= 2
        lo = jnp.maximum(tpos - w // 2, 0)
        hi = jnp.minimum(tpos - w // 2 + w, seq)
        cnt = (hi - lo).astype(F32)
        pooled = (acc[SUB:SUB + TM] / cnt - xg[SUB:SUB + TM]).astype(BF16)
        y = jnp.dot(pooled, wpool_ref[g], preferred_element_type=F32)
        y = y * pscale_ref[:, g * POOL_GROUP:(g + 1) * POOL_GROUP]
        mixed[:, ATT_WIDTH + g * POOL_GROUP:ATT_WIDTH + (g + 1) * POOL_GROUP] = y.astype(BF16)

    out = jnp.dot(mixed[...], wout_ref[...], preferred_element_type=F32)
    o_ref[...] = x_ref[...] + mod_ref[0][2:3] * out


def _mix(x2, qx, k, v, u, mod3, boff, seq, bias, w_pool, pscale, w_out):
    n = x2.shape[0]
    tps = seq // TM
    rows = seq // GRID_W
    hb = TM // HALO_TOK
    n_hb = n // HALO_TOK
    ub = TM // SUB16
    n_ub = n // SUB16
    cur = lambda i: (i, 0)
    kprev = lambda i: (jnp.maximum(i * hb - 1, 0), 0)
    knext = lambda i: (jnp.minimum((i + 1) * hb, n_hb - 1), 0)
    uprev = lambda i: (jnp.maximum(i * ub - 1, 0), 0)
    unext = lambda i: (jnp.minimum((i + 1) * ub, n_ub - 1), 0)
    kernel = functools.partial(_mix_kernel, tps=tps, rows=rows, seq=seq)
    return pl.pallas_call(
        kernel,
        grid=(n // TM,),
        in_specs=[pl.BlockSpec((TM, 2 * ATT_WIDTH), cur),
                  pl.BlockSpec((HALO_TOK, ATT_WIDTH), kprev),
                  pl.BlockSpec((TM, ATT_WIDTH), cur),
                  pl.BlockSpec((HALO_TOK, ATT_WIDTH), knext),
                  pl.BlockSpec((HALO_TOK, ATT_WIDTH), kprev),
                  pl.BlockSpec((TM, ATT_WIDTH), cur),
                  pl.BlockSpec((HALO_TOK, ATT_WIDTH), knext),
                  pl.BlockSpec((SUB16, POOL_WIDTH), uprev),
                  pl.BlockSpec((TM, POOL_WIDTH), cur),
                  pl.BlockSpec((SUB16, POOL_WIDTH), unext),
                  pl.BlockSpec((TM, D_MODEL), cur),
                  pl.BlockSpec((1, 6, D_MODEL), lambda i: (boff + i // tps, 0, 0)),
                  _const_spec(bias.shape),
                  _const_spec(w_pool.shape),
                  _const_spec((1, POOL_WIDTH)),
                  _const_spec(w_out.shape)],
        out_specs=pl.BlockSpec((TM, D_MODEL), cur),
        out_shape=jax.ShapeDtypeStruct((n, D_MODEL), F32),
        scratch_shapes=[pltpu.VMEM((TM + 2 * HALO_TOK, ATT_WIDTH), BF16),
                        pltpu.VMEM((TM + 2 * HALO_TOK, ATT_WIDTH), BF16),
                        pltpu.VMEM((TM, D_MODEL), BF16)],
        compiler_params=_params(),
        name="mix",
    )(qx, k, k, k, v, v, v, u, u, u, x2, mod3, bias, w_pool, pscale, w_out)


def _mlp_kernel(xp_ref, xc_ref, xn_ref, mod_ref, g_ref, wup_ref, cw_ref, cb_ref, wdown_ref,
                o_ref, act, *, tps):
    ti = lax.rem(pl.program_id(0), tps)
    mod = mod_ref[0]
    shift, scale, gate = mod[3:4], mod[4:5], mod[5:6]
    g = g_ref[...]
    xc = xc_ref[...]
    hp = jnp.where(ti > 0, _norm_mod(xp_ref[...], g, scale, shift), 0.0)
    hn = jnp.where(ti < tps - 1, _norm_mod(xn_ref[...], g, scale, shift), 0.0)
    h = jnp.concatenate([hp, _norm_mod(xc, g, scale, shift), hn], axis=0).astype(BF16)
    ext = TM + 2 * SUB

    def conv(cols):
        up = jnp.dot(h, wup_ref[:, cols], preferred_element_type=F32)
        cw = cw_ref[:, cols]
        prev = pltpu.roll(up, 1, 0)[SUB:SUB + TM]
        nxt = pltpu.roll(up, ext - 1, 0)[SUB:SUB + TM]
        return prev * cw[0:1] + up[SUB:SUB + TM] * cw[1:2] + nxt * cw[2:3] + cb_ref[:, cols]

    for c in range(D_FF // FC):
        gt = conv(slice(c * FC, (c + 1) * FC))
        vl = conv(slice(D_FF + c * FC, D_FF + (c + 1) * FC))
        act[:, c * FC:(c + 1) * FC] = ((gt * jax.nn.sigmoid(gt)) * vl).astype(BF16)

    y = jnp.dot(act[...], wdown_ref[...], preferred_element_type=F32)
    o_ref[...] = xc + gate * y


def _mlp(x1, mod3, boff, seq, g2, w_up, conv_w, conv_b, w_down):
    n = x1.shape[0]
    tps = seq // TM
    hb = TM // SUB
    n_hb = n // SUB
    kernel = functools.partial(_mlp_kernel, tps=tps)
    return pl.pallas_call(
        kernel,
        grid=(n // TM,),
        in_specs=[pl.BlockSpec((SUB, D_MODEL), lambda i: (jnp.maximum(i * hb - 1, 0), 0)),
                  pl.BlockSpec((TM, D_MODEL), lambda i: (i, 0)),
                  pl.BlockSpec((SUB, D_MODEL), lambda i: (jnp.minimum((i + 1) * hb, n_hb - 1), 0)),
                  pl.BlockSpec((1, 6, D_MODEL), lambda i: (boff + i // tps, 0, 0)),
                  _const_spec((1, D_MODEL)),
                  _const_spec(w_up.shape),
                  _const_spec(conv_w.shape),
                  _const_spec((1, 2 * D_FF)),
                  _const_spec(w_down.shape)],
        out_specs=pl.BlockSpec((TM, D_MODEL), lambda i: (i, 0)),
        out_shape=jax.ShapeDtypeStruct((n, D_MODEL), F32),
        scratch_shapes=[pltpu.VMEM((TM, D_FF), BF16)],
        compiler_params=_params(),
        name="mlp",
    )(x1, x1, x1, mod3, g2, w_up, conv_w, conv_b, w_down)


def _bias_table(rpb):
    cols = np.arange(GRID_W)
    start = np.clip(cols - WIN_COLS // 2, 0, GRID_W - WIN_COLS)
    onehot = np.zeros((2 * WIN_COLS - 1, GRID_W, GRID_W), np.float32)
    inside = np.zeros((GRID_W, GRID_W), bool)
    for c in cols:
        for j in range(WIN_COLS):
            c2 = start[c] + j
            onehot[c2 - c + WIN_COLS - 1, c, c2] = 1.0
            inside[c, c2] = True
    t = jnp.einsum('hrd,dcx->hrcx', rpb.astype(F32), jnp.asarray(onehot),
                   precision=lax.Precision.HIGHEST)
    t = jnp.where(jnp.asarray(inside), t, NEG)
    tabs = []
    for v in range(WIN_ROWS):
        tv = t[:, WIN_ROWS - 1 - v:2 * WIN_ROWS - 1 - v]
        tv = tv.transpose(0, 2, 1, 3).reshape(N_HEADS, GRID_W, WIN_ROWS * GRID_W)
        tabs.append(tv.reshape(N_PAIRS, 2 * GRID_W, WIN_ROWS * GRID_W))
    return jnp.stack(tabs)


def _layer(xs, mod3, p):
    (norm1_g, norm2_g, w_in, q_norm_g, k_norm_g, rpb, w_pool, pool_scale, w_out,
     w_up, conv_w, conv_b, w_down) = p
    g1 = norm1_g.reshape(1, D_MODEL)
    g2 = norm2_g.reshape(1, D_MODEL)
    w_in_b = w_in.astype(BF16)
    heads = np.arange(MXU_N) // HEAD_DIM
    seg = jnp.asarray((heads[:, None] == heads[None, :]).astype(np.float32) / HEAD_DIM, BF16)
    qg = (jnp.tile(q_norm_g, N_HEADS) * HEAD_DIM ** -0.5).reshape(1, ATT_WIDTH)
    kg = jnp.tile(k_norm_g, N_HEADS).reshape(1, ATT_WIDTH)
    bias = _bias_table(rpb)
    w_pool_b = w_pool.astype(BF16)
    pscale = pool_scale.reshape(1, POOL_WIDTH)
    w_out_b = w_out.astype(BF16)
    w_up_b = w_up.astype(BF16)
    cb = conv_b.reshape(1, 2 * D_FF)
    w_down_b = w_down.astype(BF16)

    outs = []
    boff = 0
    for x in xs:
        b, seq, _ = x.shape
        x2 = x.reshape(b * seq, D_MODEL)
        qx, k, v, u = _inproj(x2, mod3, boff, seq, g1, w_in_b, seg, qg, kg)
        x1 = _mix(x2, qx, k, v, u, mod3, boff, seq, bias, w_pool_b, pscale, w_out_b)
        y = _mlp(x1, mod3, boff, seq, g2, w_up_b, conv_w, cb, w_down_b)
        outs.append(y.reshape(b, seq, D_MODEL))
        boff += b
    return outs


def kernel(x_prompt, x_sample, c_prompt, c_sample, w_ada, b_ada, norm1_g, norm2_g, w_in,
           q_norm_g, k_norm_g, rpb, w_pool, pool_scale, w_out, w_up, conv_w, conv_b, w_down):
    xs = [x_prompt, x_sample]
    c = jnp.concatenate([c_prompt, c_sample], axis=0)
    n_b = c.shape[0]
    c = jnp.pad(c, ((0, -n_b % SUB), (0, 0)))
    for l in range(w_ada.shape[0]):
        mod = _modulation(c, w_ada[l], b_ada[l])
        mod3 = mod.reshape(c.shape[0], 6, D_MODEL)
        p = (norm1_g[l], norm2_g[l], w_in[l], q_norm_g[l], k_norm_g[l], rpb[l], w_pool[l],
             pool_scale[l], w_out[l], w_up[l], conv_w[l], conv_b[l], w_down[l])
        xs = _layer(xs, mod3, p)
    return (xs[0], xs[1])
```

```python
import functools

import numpy as np
import jax
import jax.numpy as jnp
from jax import lax
from jax.experimental import pallas as pl
from jax.experimental.pallas import tpu as pltpu

F32 = jnp.float32
BF16 = jnp.bfloat16

D_MODEL = 1024
GRID_W = 64
HEAD_DIM = 64
N_HEADS = 8
N_PAIRS = N_HEADS // 2
ATT_WIDTH = N_HEADS * HEAD_DIM
POOL_WINDOWS = (2, 4, 8, 16)
POOL_GROUP = 128
POOL_WIDTH = POOL_GROUP * len(POOL_WINDOWS)
WIN_ROWS = 8
WIN_COLS = 16
D_FF = 2816
EPS = 1e-6
NEG = -1e30

LANES = 128
MXU_N = 256
TM = 512
ROWS_PER_TILE = TM // GRID_W
TM_MLP = 1024
HALO_ROWS = 4
HALO_TOK = HALO_ROWS * GRID_W
SUB = 8
SUB16 = 16
FC = 256
VMEM_LIMIT = 56 * 1024 * 1024


def _const_spec(shape):
    nd = len(shape)
    return pl.BlockSpec(shape, lambda i: (0,) * nd, pipeline_mode=pl.Buffered(1))


def _params(vmem=VMEM_LIMIT):
    return pltpu.CompilerParams(dimension_semantics=("arbitrary",), vmem_limit_bytes=vmem)


def _ada_kernel(c_ref, w_ref, b_ref, o_ref):
    c = c_ref[...]
    s = c * jax.nn.sigmoid(c)
    o_ref[...] = jnp.dot(s, w_ref[...], preferred_element_type=F32,
                         precision=lax.Precision.HIGHEST) + b_ref[...]


def _modulation(c, w_ada, b_ada):
    rows = c.shape[0]
    n = w_ada.shape[1]
    bn = 1536
    return pl.pallas_call(
        _ada_kernel,
        grid=(n // bn,),
        in_specs=[pl.BlockSpec((rows, D_MODEL), lambda j: (0, 0)),
                  pl.BlockSpec((D_MODEL, bn), lambda j: (0, j)),
                  pl.BlockSpec((1, bn), lambda j: (0, j))],
        out_specs=pl.BlockSpec((rows, bn), lambda j: (0, j)),
        out_shape=jax.ShapeDtypeStruct((rows, n), F32),
        compiler_params=_params(),
        name="modulation",
    )(c, w_ada, b_ada.reshape(1, n))


def _norm_mod(x, g, scale, shift):
    ms = jnp.mean(x * x, axis=-1, keepdims=True)
    return ((x * lax.rsqrt(ms + EPS)) * g) * (1.0 + scale) + shift


def _inproj_kernel(x_ref, mod_ref, g_ref, w_ref, seg_ref, qg_ref, kg_ref,
                   qx_ref, k_ref, v_ref, u_ref):
    mod = mod_ref[0]
    h = _norm_mod(x_ref[...], g_ref[...], mod[1:2], mod[0:1]).astype(BF16)
    lane = lax.broadcasted_iota(jnp.int32, (TM, LANES), 1)
    first_head = lane < HEAD_DIM

    def proj(c):
        return jnp.dot(h, w_ref[:, c * MXU_N:(c + 1) * MXU_N], preferred_element_type=F32)

    def head_norm(y, gain):
        ms = jnp.dot((y * y).astype(BF16), seg_ref[...], preferred_element_type=F32)
        return (y * lax.rsqrt(ms + EPS)) * gain

    n_chunks = ATT_WIDTH // MXU_N
    qk = [proj(c) for c in range(2 * n_chunks)]
    for j in range(2 * n_chunks):
        other = proj(2 * n_chunks + j).astype(BF16)
        if j < n_chunks:
            v_ref[:, j * MXU_N:(j + 1) * MXU_N] = other
            qn = head_norm(qk[j], qg_ref[:, j * MXU_N:(j + 1) * MXU_N])
            for pp in range(MXU_N // LANES):
                pair = qn[:, pp * LANES:(pp + 1) * LANES]
                base = (j * (MXU_N // LANES) + pp) * 2 * LANES
                qx_ref[:, base:base + LANES] = jnp.where(first_head, pair, 0.0).astype(BF16)
                qx_ref[:, base + LANES:base + 2 * LANES] = jnp.where(first_head, 0.0, pair).astype(BF16)
        else:
            c = j - n_chunks
            u_ref[:, c * MXU_N:(c + 1) * MXU_N] = other
            kn = head_norm(qk[j], kg_ref[:, c * MXU_N:(c + 1) * MXU_N])
            k_ref[:, c * MXU_N:(c + 1) * MXU_N] = kn.astype(BF16)


def _inproj(x2, mod3, boff, seq, g1, w_in, seg, qg, kg):
    n = x2.shape[0]
    tps = seq // TM
    out_w = (2 * ATT_WIDTH, ATT_WIDTH, ATT_WIDTH, POOL_WIDTH)
    return pl.pallas_call(
        _inproj_kernel,
        grid=(n // TM,),
        in_specs=[pl.BlockSpec((TM, D_MODEL), lambda i: (i, 0)),
                  pl.BlockSpec((1, 6, D_MODEL), lambda i: (boff + i // tps, 0, 0)),
                  _const_spec((1, D_MODEL)),
                  _const_spec(w_in.shape),
                  _const_spec(seg.shape),
                  _const_spec((1, ATT_WIDTH)),
                  _const_spec((1, ATT_WIDTH))],
        out_specs=[pl.BlockSpec((TM, w), lambda i: (i, 0)) for w in out_w],
        out_shape=[jax.ShapeDtypeStruct((n, w), BF16) for w in out_w],
        compiler_params=_params(),
        name="inproj",
    )(x2, mod3, g1, w_in, seg, qg, kg)


def _mix_kernel(q_ref, kp_ref, kc_ref, kn_ref, vp_ref, vc_ref, vn_ref,
                up_ref, uc_ref, un_ref, x_ref, mod_ref, bias_ref,
                wpool_ref, pscale_ref, wout_ref, o_ref,
                kbuf, vbuf, mixed, sbuf_a, sbuf_b, *, tps, rows, seq):
    ti = lax.rem(pl.program_id(0), tps)
    r0 = ti * ROWS_PER_TILE

    kbuf[0:HALO_TOK, :] = kp_ref[...]
    kbuf[HALO_TOK:HALO_TOK + TM, :] = kc_ref[...]
    kbuf[HALO_TOK + TM:, :] = kn_ref[...]
    vbuf[0:HALO_TOK, :] = vp_ref[...]
    vbuf[HALO_TOK:HALO_TOK + TM, :] = vc_ref[...]
    vbuf[HALO_TOK + TM:, :] = vn_ref[...]

    lane = lax.broadcasted_iota(jnp.int32, (GRID_W, LANES), 1)
    n_keys = WIN_ROWS * GRID_W

    def window(lr):
        r = r0 + lr
        rs = jnp.clip(r - WIN_ROWS // 2, 0, rows - WIN_ROWS)
        off = pl.multiple_of((rs - r0 + HALO_ROWS) * GRID_W, GRID_W)
        return r - rs, off, pl.multiple_of(lr * GRID_W, GRID_W)

    def scores(win, dst, p):
        var, off, qoff = win
        qp = q_ref[pl.ds(qoff, GRID_W), p * 2 * LANES:(p + 1) * 2 * LANES]
        q2 = jnp.concatenate([qp[:, :LANES], qp[:, LANES:]], axis=0)
        kw = kbuf[pl.ds(off, n_keys), p * LANES:(p + 1) * LANES]
        s = lax.dot_general(q2, kw, (((1,), (1,)), ((), ())), preferred_element_type=F32)
        dst[p] = s + bias_ref[var, p]

    def attend(win, src, p):
        _, off, qoff = win
        s = src[p]
        m = jnp.max(s, axis=-1, keepdims=True)
        e = jnp.exp(s - m)
        l = jnp.sum(e, axis=-1, keepdims=True)
        vw = vbuf[pl.ds(off, n_keys), p * LANES:(p + 1) * LANES]
        o2 = jnp.dot(e.astype(BF16), vw, preferred_element_type=F32) / l
        o = jnp.where(lane < HEAD_DIM, o2[:GRID_W], o2[GRID_W:])
        mixed[pl.ds(qoff, GRID_W), p * LANES:(p + 1) * LANES] = o.astype(BF16)

    def row_step(lr, src, dst):
        cur, nxt = window(lr), window(lr + 1)
        scores(nxt, dst, 0)
        scores(nxt, dst, 1)
        for p in range(N_PAIRS):
            attend(cur, src, p)
            if p + 2 < N_PAIRS:
                scores(nxt, dst, p + 2)

    for p in range(N_PAIRS):
        scores(window(0), sbuf_a, p)

    def two_rows(j, carry):
        row_step(2 * j, sbuf_a, sbuf_b)
        row_step(2 * j + 1, sbuf_b, sbuf_a)
        return carry

    lax.fori_loop(0, ROWS_PER_TILE // 2 - 1, two_rows, 0)
    row_step(ROWS_PER_TILE - 2, sbuf_a, sbuf_b)
    for p in range(N_PAIRS):
        attend(window(ROWS_PER_TILE - 1), sbuf_b, p)

    uprev = jnp.where(ti > 0, up_ref[...].astype(F32)[SUB:], 0.0)
    unext = jnp.where(ti < tps - 1, un_ref[...].astype(F32)[:SUB], 0.0)
    ue = jnp.concatenate([uprev, uc_ref[...].astype(F32), unext], axis=0)
    ext = TM + 2 * SUB
    tpos = ti * TM + lax.broadcasted_iota(jnp.int32, (TM, POOL_GROUP), 0)
    for g, w in enumerate(POOL_WINDOWS):
        xg = ue[:, g * POOL_GROUP:(g + 1) * POOL_GROUP]
        acc = xg + pltpu.roll(xg, 1, 0)
        half = 1
        while 2 * half < w:
            acc = pltpu.roll(acc, half, 0) + pltpu.roll(acc, ext - half, 0)
            half *= 2
        lo = jnp.maximum(tpos - w // 2, 0)
        hi = jnp.minimum(tpos - w // 2 + w, seq)
        cnt = (hi - lo).astype(F32)
        pooled = (acc[SUB:SUB + TM] / cnt - xg[SUB:SUB + TM]).astype(BF16)
        y = jnp.dot(pooled, wpool_ref[g], preferred_element_type=F32)
        y = y * pscale_ref[:, g * POOL_GROUP:(g + 1) * POOL_GROUP]
        mixed[:, ATT_WIDTH + g * POOL_GROUP:ATT_WIDTH + (g + 1) * POOL_GROUP] = y.astype(BF16)

    out = jnp.dot(mixed[...], wout_ref[...], preferred_element_type=F32)
    o_ref[...] = x_ref[...] + mod_ref[0][2:3] * out


def _mix(x2, qx, k, v, u, mod3, boff, seq, bias, w_pool, pscale, w_out):
    n = x2.shape[0]
    tps = seq // TM
    rows = seq // GRID_W
    hb = TM // HALO_TOK
    n_hb = n // HALO_TOK
    ub = TM // SUB16
    n_ub = n // SUB16
    cur = lambda i: (i, 0)
    kprev = lambda i: (jnp.maximum(i * hb - 1, 0), 0)
    knext = lambda i: (jnp.minimum((i + 1) * hb, n_hb - 1), 0)
    uprev = lambda i: (jnp.maximum(i * ub - 1, 0), 0)
    unext = lambda i: (jnp.minimum((i + 1) * ub, n_ub - 1), 0)
    kernel = functools.partial(_mix_kernel, tps=tps, rows=rows, seq=seq)
    return pl.pallas_call(
        kernel,
        grid=(n // TM,),
        in_specs=[pl.BlockSpec((TM, 2 * ATT_WIDTH), cur),
                  pl.BlockSpec((HALO_TOK, ATT_WIDTH), kprev),
                  pl.BlockSpec((TM, ATT_WIDTH), cur),
                  pl.BlockSpec((HALO_TOK, ATT_WIDTH), knext),
                  pl.BlockSpec((HALO_TOK, ATT_WIDTH), kprev),
                  pl.BlockSpec((TM, ATT_WIDTH), cur),
                  pl.BlockSpec((HALO_TOK, ATT_WIDTH), knext),
                  pl.BlockSpec((SUB16, POOL_WIDTH), uprev),
                  pl.BlockSpec((TM, POOL_WIDTH), cur),
                  pl.BlockSpec((SUB16, POOL_WIDTH), unext),
                  pl.BlockSpec((TM, D_MODEL), cur),
                  pl.BlockSpec((1, 6, D_MODEL), lambda i: (boff + i // tps, 0, 0)),
                  _const_spec(bias.shape),
                  _const_spec(w_pool.shape),
                  _const_spec((1, POOL_WIDTH)),
                  _const_spec(w_out.shape)],
        out_specs=pl.BlockSpec((TM, D_MODEL), cur),
        out_shape=jax.ShapeDtypeStruct((n, D_MODEL), F32),
        scratch_shapes=[pltpu.VMEM((TM + 2 * HALO_TOK, ATT_WIDTH), BF16),
                        pltpu.VMEM((TM + 2 * HALO_TOK, ATT_WIDTH), BF16),
                        pltpu.VMEM((TM, D_MODEL), BF16),
                        pltpu.VMEM((N_PAIRS, 2 * GRID_W, WIN_ROWS * GRID_W), F32),
                        pltpu.VMEM((N_PAIRS, 2 * GRID_W, WIN_ROWS * GRID_W), F32)],
        compiler_params=_params(),
        name="mix",
    )(qx, k, k, k, v, v, v, u, u, u, x2, mod3, bias, w_pool, pscale, w_out)


def _mlp_kernel(xp_ref, xc_ref, xn_ref, mod_ref, g_ref, wup_ref, cw_ref, cb_ref, wdown_ref,
                o_ref, act, *, tps, tm):
    ti = lax.rem(pl.program_id(0), tps)
    mod = mod_ref[0]
    shift, scale, gate = mod[3:4], mod[4:5], mod[5:6]
    g = g_ref[...]
    xc = xc_ref[...]
    hp = jnp.where(ti > 0, _norm_mod(xp_ref[...], g, scale, shift), 0.0)
    hn = jnp.where(ti < tps - 1, _norm_mod(xn_ref[...], g, scale, shift), 0.0)
    h = jnp.concatenate([hp, _norm_mod(xc, g, scale, shift), hn], axis=0).astype(BF16)
    ext = tm + 2 * SUB

    def conv(cols):
        up = jnp.dot(h, wup_ref[:, cols], preferred_element_type=F32)
        cw = cw_ref[:, cols]
        prev = pltpu.roll(up, 1, 0)[SUB:SUB + tm]
        nxt = pltpu.roll(up, ext - 1, 0)[SUB:SUB + tm]
        return prev * cw[0:1] + up[SUB:SUB + tm] * cw[1:2] + nxt * cw[2:3] + cb_ref[:, cols]

    for c in range(D_FF // FC):
        gt = conv(slice(c * FC, (c + 1) * FC))
        vl = conv(slice(D_FF + c * FC, D_FF + (c + 1) * FC))
        act[:, c * FC:(c + 1) * FC] = ((gt * jax.nn.sigmoid(gt)) * vl).astype(BF16)

    y = jnp.dot(act[...], wdown_ref[...], preferred_element_type=F32)
    o_ref[...] = xc + gate * y


def _mlp(x1, mod3, boff, seq, g2, w_up, conv_w, conv_b, w_down):
    n = x1.shape[0]
    tm = TM_MLP
    tps = seq // tm
    hb = tm // SUB
    n_hb = n // SUB
    kernel = functools.partial(_mlp_kernel, tps=tps, tm=tm)
    return pl.pallas_call(
        kernel,
        grid=(n // tm,),
        in_specs=[pl.BlockSpec((SUB, D_MODEL), lambda i: (jnp.maximum(i * hb - 1, 0), 0)),
                  pl.BlockSpec((tm, D_MODEL), lambda i: (i, 0)),
                  pl.BlockSpec((SUB, D_MODEL), lambda i: (jnp.minimum((i + 1) * hb, n_hb - 1), 0)),
                  pl.BlockSpec((1, 6, D_MODEL), lambda i: (boff + i // tps, 0, 0)),
                  _const_spec((1, D_MODEL)),
                  _const_spec(w_up.shape),
                  _const_spec(conv_w.shape),
                  _const_spec((1, 2 * D_FF)),
                  _const_spec(w_down.shape)],
        out_specs=pl.BlockSpec((tm, D_MODEL), lambda i: (i, 0)),
        out_shape=jax.ShapeDtypeStruct((n, D_MODEL), F32),
        scratch_shapes=[pltpu.VMEM((tm, D_FF), BF16)],
        compiler_params=_params(),
        name="mlp",
    )(x1, x1, x1, mod3, g2, w_up, conv_w, conv_b, w_down)


def _bias_table(rpb):
    cols = np.arange(GRID_W)
    start = np.clip(cols - WIN_COLS // 2, 0, GRID_W - WIN_COLS)
    onehot = np.zeros((2 * WIN_COLS - 1, GRID_W, GRID_W), np.float32)
    inside = np.zeros((GRID_W, GRID_W), bool)
    for c in cols:
        for j in range(WIN_COLS):
            c2 = start[c] + j
            onehot[c2 - c + WIN_COLS - 1, c, c2] = 1.0
            inside[c, c2] = True
    t = jnp.einsum('hrd,dcx->hrcx', rpb.astype(F32), jnp.asarray(onehot),
                   precision=lax.Precision.HIGHEST)
    t = jnp.where(jnp.asarray(inside), t, NEG)
    tabs = []
    for v in range(WIN_ROWS):
        tv = t[:, WIN_ROWS - 1 - v:2 * WIN_ROWS - 1 - v]
        tv = tv.transpose(0, 2, 1, 3).reshape(N_HEADS, GRID_W, WIN_ROWS * GRID_W)
        tabs.append(tv.reshape(N_PAIRS, 2 * GRID_W, WIN_ROWS * GRID_W))
    return jnp.stack(tabs)


def _layer(xs, mod3, p):
    (norm1_g, norm2_g, w_in, q_norm_g, k_norm_g, rpb, w_pool, pool_scale, w_out,
     w_up, conv_w, conv_b, w_down) = p
    g1 = norm1_g.reshape(1, D_MODEL)
    g2 = norm2_g.reshape(1, D_MODEL)
    w_in_b = w_in.astype(BF16)
    heads = np.arange(MXU_N) // HEAD_DIM
    seg = jnp.asarray((heads[:, None] == heads[None, :]).astype(np.float32) / HEAD_DIM, BF16)
    qg = (jnp.tile(q_norm_g, N_HEADS) * HEAD_DIM ** -0.5).reshape(1, ATT_WIDTH)
    kg = jnp.tile(k_norm_g, N_HEADS).reshape(1, ATT_WIDTH)
    bias = _bias_table(rpb)
    w_pool_b = w_pool.astype(BF16)
    pscale = pool_scale.reshape(1, POOL_WIDTH)
    w_out_b = w_out.astype(BF16)
    w_up_b = w_up.astype(BF16)
    cb = conv_b.reshape(1, 2 * D_FF)
    w_down_b = w_down.astype(BF16)

    outs = []
    boff = 0
    for x in xs:
        b, seq, _ = x.shape
        x2 = x.reshape(b * seq, D_MODEL)
        qx, k, v, u = _inproj(x2, mod3, boff, seq, g1, w_in_b, seg, qg, kg)
        x1 = _mix(x2, qx, k, v, u, mod3, boff, seq, bias, w_pool_b, pscale, w_out_b)
        y = _mlp(x1, mod3, boff, seq, g2, w_up_b, conv_w, cb, w_down_b)
        outs.append(y.reshape(b, seq, D_MODEL))
        boff += b
    return outs


def kernel(x_prompt, x_sample, c_prompt, c_sample, w_ada, b_ada, norm1_g, norm2_g, w_in,
           q_norm_g, k_norm_g, rpb, w_pool, pool_scale, w_out, w_up, conv_w, conv_b, w_down):
    xs = [x_prompt, x_sample]
    c = jnp.concatenate([c_prompt, c_sample], axis=0)
    n_b = c.shape[0]
    c = jnp.pad(c, ((0, -n_b % SUB), (0, 0)))
    for l in range(w_ada.shape[0]):
        mod = _modulation(c, w_ada[l], b_ada[l])
        mod3 = mod.reshape(c.shape[0], 6, D_MODEL)
        p = (norm1_g[l], norm2_g[l], w_in[l], q_norm_g[l], k_norm_g[l], rpb[l], w_pool[l],
             pool_scale[l], w_out[l], w_up[l], conv_w[l], conv_b[l], w_down[l])
        xs = _layer(xs, mod3, p)
    return (xs[0], xs[1])
```

```python
import functools

import numpy as np
import jax
import jax.numpy as jnp
from jax import lax
from jax.experimental import pallas as pl
from jax.experimental.pallas import tpu as pltpu

F32 = jnp.float32
BF16 = jnp.bfloat16

D_MODEL = 1024
GRID_W = 64
HEAD_DIM = 64
N_HEADS = 8
N_PAIRS = N_HEADS // 2
ATT_WIDTH = N_HEADS * HEAD_DIM
POOL_WINDOWS = (2, 4, 8, 16)
POOL_GROUP = 128
POOL_WIDTH = POOL_GROUP * len(POOL_WINDOWS)
WIN_ROWS = 8
WIN_COLS = 16
D_FF = 2816
EPS = 1e-6
NEG = -1e30

LANES = 128
MXU_N = 256
TM = 512
ROWS_PER_TILE = TM // GRID_W
TM_MLP = 1024
TM_IN = 1024
HALO_ROWS = 4
HALO_TOK = HALO_ROWS * GRID_W
SUB = 8
SUB16 = 16
FC = 256
VMEM_LIMIT = 56 * 1024 * 1024


def _const_spec(shape):
    nd = len(shape)
    return pl.BlockSpec(shape, lambda i: (0,) * nd, pipeline_mode=pl.Buffered(1))


def _params(vmem=VMEM_LIMIT):
    return pltpu.CompilerParams(dimension_semantics=("arbitrary",), vmem_limit_bytes=vmem)


def _ada_kernel(c_ref, w_ref, b_ref, o_ref):
    c = c_ref[...]
    s = c * jax.nn.sigmoid(c)
    hi = s.astype(BF16)
    lo = (s - hi.astype(F32)).astype(BF16)
    r = jnp.dot(jnp.concatenate([hi, lo], axis=0), w_ref[...].astype(BF16),
                preferred_element_type=F32)
    n = c.shape[0]
    o_ref[...] = (r[:n] + r[n:]) + b_ref[...]


def _modulation(c, w_ada, b_ada):
    rows = c.shape[0]
    n = w_ada.shape[1]
    bn = 1536
    return pl.pallas_call(
        _ada_kernel,
        grid=(n // bn,),
        in_specs=[pl.BlockSpec((rows, D_MODEL), lambda j: (0, 0)),
                  pl.BlockSpec((D_MODEL, bn), lambda j: (0, j)),
                  pl.BlockSpec((1, bn), lambda j: (0, j))],
        out_specs=pl.BlockSpec((rows, bn), lambda j: (0, j)),
        out_shape=jax.ShapeDtypeStruct((rows, n), F32),
        compiler_params=_params(),
        name="modulation",
    )(c, w_ada, b_ada.reshape(1, n))


def _norm_mod(x, g, scale, shift):
    ms = jnp.mean(x * x, axis=-1, keepdims=True)
    return ((x * lax.rsqrt(ms + EPS)) * g) * (1.0 + scale) + shift


def _inproj_kernel(x_ref, mod_ref, g_ref, w_ref, seg_ref, qg_ref, kg_ref,
                   qx_ref, k_ref, v_ref, u_ref):
    mod = mod_ref[0]
    h = _norm_mod(x_ref[...], g_ref[...], mod[1:2], mod[0:1]).astype(BF16)
    lane = lax.broadcasted_iota(jnp.int32, (TM_IN, LANES), 1)
    first_head = lane < HEAD_DIM

    def proj(c):
        return jnp.dot(h, w_ref[:, c * MXU_N:(c + 1) * MXU_N], preferred_element_type=F32)

    def head_norm(y, gain):
        ms = jnp.dot((y * y).astype(BF16), seg_ref[...], preferred_element_type=F32)
        return (y * lax.rsqrt(ms + EPS)) * gain

    n_chunks = ATT_WIDTH // MXU_N
    qk = [proj(c) for c in range(2 * n_chunks)]
    for j in range(2 * n_chunks):
        other = proj(2 * n_chunks + j).astype(BF16)
        if j < n_chunks:
            v_ref[:, j * MXU_N:(j + 1) * MXU_N] = other
            qn = head_norm(qk[j], qg_ref[:, j * MXU_N:(j + 1) * MXU_N])
            for pp in range(MXU_N // LANES):
                pair = qn[:, pp * LANES:(pp + 1) * LANES]
                base = (j * (MXU_N // LANES) + pp) * 2 * LANES
                qx_ref[:, base:base + LANES] = jnp.where(first_head, pair, 0.0).astype(BF16)
                qx_ref[:, base + LANES:base + 2 * LANES] = jnp.where(first_head, 0.0, pair).astype(BF16)
        else:
            c = j - n_chunks
            u_ref[:, c * MXU_N:(c + 1) * MXU_N] = other
            kn = head_norm(qk[j], kg_ref[:, c * MXU_N:(c + 1) * MXU_N])
            k_ref[:, c * MXU_N:(c + 1) * MXU_N] = kn.astype(BF16)


def _inproj(x2, mod3, boff, seq, g1, w_in, seg, qg, kg):
    n = x2.shape[0]
    tps = seq // TM_IN
    out_w = (2 * ATT_WIDTH, ATT_WIDTH, ATT_WIDTH, POOL_WIDTH)
    return pl.pallas_call(
        _inproj_kernel,
        grid=(n // TM_IN,),
        in_specs=[pl.BlockSpec((TM_IN, D_MODEL), lambda i: (i, 0)),
                  pl.BlockSpec((1, 6, D_MODEL), lambda i: (boff + i // tps, 0, 0)),
                  _const_spec((1, D_MODEL)),
                  _const_spec(w_in.shape),
                  _const_spec(seg.shape),
                  _const_spec((1, ATT_WIDTH)),
                  _const_spec((1, ATT_WIDTH))],
        out_specs=[pl.BlockSpec((TM_IN, w), lambda i: (i, 0)) for w in out_w],
        out_shape=[jax.ShapeDtypeStruct((n, w), BF16) for w in out_w],
        compiler_params=_params(),
        name="inproj",
    )(x2, mod3, g1, w_in, seg, qg, kg)


def _mix_kernel(q_ref, kp_ref, kc_ref, kn_ref, vp_ref, vc_ref, vn_ref,
                up_ref, uc_ref, un_ref, x_ref, mod_ref, bias_ref,
                wpool_ref, pscale_ref, wout_ref, o_ref,
                kbuf, vbuf, mixed, sbuf_a, sbuf_b, ubuf, lvl, *, tps, rows, seq):
    ti = lax.rem(pl.program_id(0), tps)
    r0 = ti * ROWS_PER_TILE

    kbuf[0:HALO_TOK, :] = kp_ref[...]
    kbuf[HALO_TOK:HALO_TOK + TM, :] = kc_ref[...]
    kbuf[HALO_TOK + TM:, :] = kn_ref[...]
    vbuf[0:HALO_TOK, :] = vp_ref[...]
    vbuf[HALO_TOK:HALO_TOK + TM, :] = vc_ref[...]
    vbuf[HALO_TOK + TM:, :] = vn_ref[...]

    lane = lax.broadcasted_iota(jnp.int32, (GRID_W, LANES), 1)
    n_keys = WIN_ROWS * GRID_W

    def window(lr):
        r = r0 + lr
        rs = jnp.clip(r - WIN_ROWS // 2, 0, rows - WIN_ROWS)
        off = pl.multiple_of((rs - r0 + HALO_ROWS) * GRID_W, GRID_W)
        return r - rs, off, pl.multiple_of(lr * GRID_W, GRID_W)

    def scores(win, dst, p):
        var, off, qoff = win
        qp = q_ref[pl.ds(qoff, GRID_W), p * 2 * LANES:(p + 1) * 2 * LANES]
        q2 = jnp.concatenate([qp[:, :LANES], qp[:, LANES:]], axis=0)
        kw = kbuf[pl.ds(off, n_keys), p * LANES:(p + 1) * LANES]
        s = lax.dot_general(q2, kw, (((1,), (1,)), ((), ())), preferred_element_type=F32)
        dst[p] = s + bias_ref[var, p]

    def attend(win, src, p):
        _, off, qoff = win
        s = src[p]
        m = jnp.max(s, axis=-1, keepdims=True)
        e = jnp.exp(s - m)
        l = jnp.sum(e, axis=-1, keepdims=True)
        vw = vbuf[pl.ds(off, n_keys), p * LANES:(p + 1) * LANES]
        o2 = jnp.dot(e.astype(BF16), vw, preferred_element_type=F32) / l
        o = jnp.where(lane < HEAD_DIM, o2[:GRID_W], o2[GRID_W:])
        mixed[pl.ds(qoff, GRID_W), p * LANES:(p + 1) * LANES] = o.astype(BF16)

    def row_step(lr, src, dst):
        cur, nxt = window(lr), window(lr + 1)
        scores(nxt, dst, 0)
        scores(nxt, dst, 1)
        for p in range(N_PAIRS):
            attend(cur, src, p)
            if p + 2 < N_PAIRS:
                scores(nxt, dst, p + 2)

    for p in range(N_PAIRS):
        scores(window(0), sbuf_a, p)

    def two_rows(j, carry):
        row_step(2 * j, sbuf_a, sbuf_b)
        row_step(2 * j + 1, sbuf_b, sbuf_a)
        return carry

    lax.fori_loop(0, ROWS_PER_TILE // 2 - 1, two_rows, 0)
    row_step(ROWS_PER_TILE - 2, sbuf_a, sbuf_b)
    for p in range(N_PAIRS):
        attend(window(ROWS_PER_TILE - 1), sbuf_b, p)

    ext = TM + 4 * SUB
    inner = TM + 2 * SUB
    centre = 2 * SUB
    ubuf[0:SUB, :] = jnp.zeros((SUB, POOL_WIDTH), F32)
    ubuf[ext - SUB:ext, :] = jnp.zeros((SUB, POOL_WIDTH), F32)
    ubuf[SUB:centre, :] = jnp.where(ti > 0, up_ref[...].astype(F32)[SUB:], 0.0)
    ubuf[centre:centre + TM, :] = uc_ref[...].astype(F32)
    ubuf[centre + TM:ext - SUB, :] = jnp.where(ti < tps - 1, un_ref[...].astype(F32)[:SUB], 0.0)
    lvl[:, :, 0:SUB, :] = jnp.zeros(lvl.shape[:2] + (SUB, POOL_GROUP), F32)
    lvl[:, :, ext - SUB:ext, :] = jnp.zeros(lvl.shape[:2] + (SUB, POOL_GROUP), F32)
    tpos = ti * TM + lax.broadcasted_iota(jnp.int32, (TM, POOL_GROUP), 0)
    for g, w in enumerate(POOL_WINDOWS):
        cols = slice(g * POOL_GROUP, (g + 1) * POOL_GROUP)
        if w == 2:
            acc = ubuf[pl.ds(centre - 1, TM), cols] + ubuf[pl.ds(centre, TM), cols]
        else:
            lv = lvl.at[g - 1]
            lv[0, SUB:SUB + inner, :] = (ubuf[pl.ds(SUB - 1, inner), cols]
                                         + ubuf[pl.ds(SUB, inner), cols])
            half, k = 1, 0
            while 4 * half < w:
                lv[1 - k, SUB:SUB + inner, :] = (lv[k, pl.ds(SUB - half, inner), :]
                                                 + lv[k, pl.ds(SUB + half, inner), :])
                half, k = 2 * half, 1 - k
            acc = lv[k, pl.ds(centre - half, TM), :] + lv[k, pl.ds(centre + half, TM), :]
        lo = jnp.maximum(tpos - w // 2, 0)
        hi = jnp.minimum(tpos - w // 2 + w, seq)
        cnt = (hi - lo).astype(F32)
        pooled = (acc / cnt - ubuf[centre:centre + TM, cols]).astype(BF16)
        y = jnp.dot(pooled, wpool_ref[g], preferred_element_type=F32)
        y = y * pscale_ref[:, g * POOL_GROUP:(g + 1) * POOL_GROUP]
        mixed[:, ATT_WIDTH + g * POOL_GROUP:ATT_WIDTH + (g + 1) * POOL_GROUP] = y.astype(BF16)

    out = jnp.dot(mixed[...], wout_ref[...], preferred_element_type=F32)
    o_ref[...] = x_ref[...] + mod_ref[0][2:3] * out


def _mix(x2, qx, k, v, u, mod3, boff, seq, bias, w_pool, pscale, w_out):
    n = x2.shape[0]
    tps = seq // TM
    rows = seq // GRID_W
    hb = TM // HALO_TOK
    n_hb = n // HALO_TOK
    ub = TM // SUB16
    n_ub = n // SUB16
    cur = lambda i: (i, 0)
    kprev = lambda i: (jnp.maximum(i * hb - 1, 0), 0)
    knext = lambda i: (jnp.minimum((i + 1) * hb, n_hb - 1), 0)
    uprev = lambda i: (jnp.maximum(i * ub - 1, 0), 0)
    unext = lambda i: (jnp.minimum((i + 1) * ub, n_ub - 1), 0)
    kernel = functools.partial(_mix_kernel, tps=tps, rows=rows, seq=seq)
    return pl.pallas_call(
        kernel,
        grid=(n // TM,),
        in_specs=[pl.BlockSpec((TM, 2 * ATT_WIDTH), cur),
                  pl.BlockSpec((HALO_TOK, ATT_WIDTH), kprev),
                  pl.BlockSpec((TM, ATT_WIDTH), cur),
                  pl.BlockSpec((HALO_TOK, ATT_WIDTH), knext),
                  pl.BlockSpec((HALO_TOK, ATT_WIDTH), kprev),
                  pl.BlockSpec((TM, ATT_WIDTH), cur),
                  pl.BlockSpec((HALO_TOK, ATT_WIDTH), knext),
                  pl.BlockSpec((SUB16, POOL_WIDTH), uprev),
                  pl.BlockSpec((TM, POOL_WIDTH), cur),
                  pl.BlockSpec((SUB16, POOL_WIDTH), unext),
                  pl.BlockSpec((TM, D_MODEL), cur),
                  pl.BlockSpec((1, 6, D_MODEL), lambda i: (boff + i // tps, 0, 0)),
                  _const_spec(bias.shape),
                  _const_spec(w_pool.shape),
                  _const_spec((1, POOL_WIDTH)),
                  _const_spec(w_out.shape)],
        out_specs=pl.BlockSpec((TM, D_MODEL), cur),
        out_shape=jax.ShapeDtypeStruct((n, D_MODEL), F32),
        scratch_shapes=[pltpu.VMEM((TM + 2 * HALO_TOK, ATT_WIDTH), BF16),
                        pltpu.VMEM((TM + 2 * HALO_TOK, ATT_WIDTH), BF16),
                        pltpu.VMEM((TM, D_MODEL), BF16),
                        pltpu.VMEM((N_PAIRS, 2 * GRID_W, WIN_ROWS * GRID_W), F32),
                        pltpu.VMEM((N_PAIRS, 2 * GRID_W, WIN_ROWS * GRID_W), F32),
                        pltpu.VMEM((TM + 4 * SUB, POOL_WIDTH), F32),
                        pltpu.VMEM((len(POOL_WINDOWS) - 1, 2, TM + 4 * SUB, POOL_GROUP), F32)],
        compiler_params=_params(),
        name="mix",
    )(qx, k, k, k, v, v, v, u, u, u, x2, mod3, bias, w_pool, pscale, w_out)


def _mlp_kernel(xp_ref, xc_ref, xn_ref, mod_ref, g_ref, wup_ref, cw_ref, cb_ref, wdown_ref,
                o_ref, act, *, tps, tm):
    ti = lax.rem(pl.program_id(0), tps)
    mod = mod_ref[0]
    shift, scale, gate = mod[3:4], mod[4:5], mod[5:6]
    g = g_ref[...]
    xc = xc_ref[...]
    hp = jnp.where(ti > 0, _norm_mod(xp_ref[...], g, scale, shift), 0.0)
    hn = jnp.where(ti < tps - 1, _norm_mod(xn_ref[...], g, scale, shift), 0.0)
    h = jnp.concatenate([hp, _norm_mod(xc, g, scale, shift), hn], axis=0).astype(BF16)
    ext = tm + 2 * SUB

    def conv(cols):
        up = jnp.dot(h, wup_ref[:, cols], preferred_element_type=F32)
        cw = cw_ref[:, cols]
        prev = pltpu.roll(up, 1, 0)[SUB:SUB + tm]
        nxt = pltpu.roll(up, ext - 1, 0)[SUB:SUB + tm]
        return prev * cw[0:1] + up[SUB:SUB + tm] * cw[1:2] + nxt * cw[2:3] + cb_ref[:, cols]

    for c in range(D_FF // FC):
        gt = conv(slice(c * FC, (c + 1) * FC))
        vl = conv(slice(D_FF + c * FC, D_FF + (c + 1) * FC))
        act[:, c * FC:(c + 1) * FC] = ((gt * jax.nn.sigmoid(gt)) * vl).astype(BF16)

    y = jnp.dot(act[...], wdown_ref[...], preferred_element_type=F32)
    o_ref[...] = xc + gate * y


def _mlp(x1, mod3, boff, seq, g2, w_up, conv_w, conv_b, w_down):
    n = x1.shape[0]
    tm = TM_MLP
    tps = seq // tm
    hb = tm // SUB
    n_hb = n // SUB
    kernel = functools.partial(_mlp_kernel, tps=tps, tm=tm)
    return pl.pallas_call(
        kernel,
        grid=(n // tm,),
        in_specs=[pl.BlockSpec((SUB, D_MODEL), lambda i: (jnp.maximum(i * hb - 1, 0), 0)),
                  pl.BlockSpec((tm, D_MODEL), lambda i: (i, 0)),
                  pl.BlockSpec((SUB, D_MODEL), lambda i: (jnp.minimum((i + 1) * hb, n_hb - 1), 0)),
                  pl.BlockSpec((1, 6, D_MODEL), lambda i: (boff + i // tps, 0, 0)),
                  _const_spec((1, D_MODEL)),
                  _const_spec(w_up.shape),
                  _const_spec(conv_w.shape),
                  _const_spec((1, 2 * D_FF)),
                  _const_spec(w_down.shape)],
        out_specs=pl.BlockSpec((tm, D_MODEL), lambda i: (i, 0)),
        out_shape=jax.ShapeDtypeStruct((n, D_MODEL), F32),
        scratch_shapes=[pltpu.VMEM((tm, D_FF), BF16)],
        compiler_params=_params(),
        name="mlp",
    )(x1, x1, x1, mod3, g2, w_up, conv_w, conv_b, w_down)


def _bias_table(rpb):
    cols = np.arange(GRID_W)
    start = np.clip(cols - WIN_COLS // 2, 0, GRID_W - WIN_COLS)
    onehot = np.zeros((2 * WIN_COLS - 1, GRID_W, GRID_W), np.float32)
    inside = np.zeros((GRID_W, GRID_W), bool)
    for c in cols:
        for j in range(WIN_COLS):
            c2 = start[c] + j
            onehot[c2 - c + WIN_COLS - 1, c, c2] = 1.0
            inside[c, c2] = True
    t = jnp.einsum('hrd,dcx->hrcx', rpb.astype(F32), jnp.asarray(onehot),
                   precision=lax.Precision.HIGHEST)
    t = jnp.where(jnp.asarray(inside), t, NEG)
    tab = jnp.stack([t[:, WIN_ROWS - 1 - v:2 * WIN_ROWS - 1 - v] for v in range(WIN_ROWS)])
    tab = tab.transpose(0, 1, 3, 2, 4)
    return tab.reshape(WIN_ROWS, N_PAIRS, 2 * GRID_W, WIN_ROWS * GRID_W)


def _layer(xs, mod3, p):
    (norm1_g, norm2_g, w_in, q_norm_g, k_norm_g, rpb, w_pool, pool_scale, w_out,
     w_up, conv_w, conv_b, w_down) = p
    g1 = norm1_g.reshape(1, D_MODEL)
    g2 = norm2_g.reshape(1, D_MODEL)
    w_in_b = w_in.astype(BF16)
    heads = np.arange(MXU_N) // HEAD_DIM
    seg = jnp.asarray((heads[:, None] == heads[None, :]).astype(np.float32) / HEAD_DIM, BF16)
    qg = (jnp.tile(q_norm_g, N_HEADS) * HEAD_DIM ** -0.5).reshape(1, ATT_WIDTH)
    kg = jnp.tile(k_norm_g, N_HEADS).reshape(1, ATT_WIDTH)
    bias = _bias_table(rpb)
    w_pool_b = w_pool.astype(BF16)
    pscale = pool_scale.reshape(1, POOL_WIDTH)
    w_out_b = w_out.astype(BF16)
    w_up_b = w_up.astype(BF16)
    cb = conv_b.reshape(1, 2 * D_FF)
    w_down_b = w_down.astype(BF16)

    outs = []
    boff = 0
    for x in xs:
        b, seq, _ = x.shape
        x2 = x.reshape(b * seq, D_MODEL)
        qx, k, v, u = _inproj(x2, mod3, boff, seq, g1, w_in_b, seg, qg, kg)
        x1 = _mix(x2, qx, k, v, u, mod3, boff, seq, bias, w_pool_b, pscale, w_out_b)
        y = _mlp(x1, mod3, boff, seq, g2, w_up_b, conv_w, cb, w_down_b)
        outs.append(y.reshape(b, seq, D_MODEL))
        boff += b
    return outs


def kernel(x_prompt, x_sample, c_prompt, c_sample, w_ada, b_ada, norm1_g, norm2_g, w_in,
           q_norm_g, k_norm_g, rpb, w_pool, pool_scale, w_out, w_up, conv_w, conv_b, w_down):
    xs = [x_prompt, x_sample]
    c = jnp.concatenate([c_prompt, c_sample], axis=0)
    n_b = c.shape[0]
    c = jnp.pad(c, ((0, -n_b % SUB16), (0, 0)))
    for l in range(w_ada.shape[0]):
        mod = _modulation(c, w_ada[l], b_ada[l])
        mod3 = mod.reshape(c.shape[0], 6, D_MODEL)
        p = (norm1_g[l], norm2_g[l], w_in[l], q_norm_g[l], k_norm_g[l], rpb[l], w_pool[l],
             pool_scale[l], w_out[l], w_up[l], conv_w[l], conv_b[l], w_down[l])
        xs = _layer(xs, mod3, p)
    return (xs[0], xs[1])
```

```python
import functools

import numpy as np
import jax
import jax.numpy as jnp
from jax import lax
from jax.experimental import pallas as pl
from jax.experimental.pallas import tpu as pltpu

F32 = jnp.float32
BF16 = jnp.bfloat16

D_MODEL = 1024
GRID_W = 64
HEAD_DIM = 64
N_HEADS = 8
N_PAIRS = N_HEADS // 2
ATT_WIDTH = N_HEADS * HEAD_DIM
POOL_WINDOWS = (2, 4, 8, 16)
POOL_GROUP = 128
POOL_WIDTH = POOL_GROUP * len(POOL_WINDOWS)
WIN_ROWS = 8
WIN_COLS = 16
D_FF = 2816
EPS = 1e-6
NEG = -1e30

LANES = 128
MXU_N = 256
TM = 1024
ROWS_PER_TILE = TM // GRID_W
TM_MLP = 1024
TM_IN = 1024
HALO_ROWS = 4
HALO_TOK = HALO_ROWS * GRID_W
SUB = 8
SUB16 = 16
FC = 256
VMEM_LIMIT = 56 * 1024 * 1024


def _const_spec(shape):
    nd = len(shape)
    return pl.BlockSpec(shape, lambda i: (0,) * nd, pipeline_mode=pl.Buffered(1))


def _params(vmem=VMEM_LIMIT):
    return pltpu.CompilerParams(dimension_semantics=("arbitrary",), vmem_limit_bytes=vmem)


def _ada_kernel(c_ref, w_ref, b_ref, o_ref):
    c = c_ref[...]
    s = c * jax.nn.sigmoid(c)
    hi = s.astype(BF16)
    lo = (s - hi.astype(F32)).astype(BF16)
    r = jnp.dot(jnp.concatenate([hi, lo], axis=0), w_ref[...].astype(BF16),
                preferred_element_type=F32)
    n = c.shape[0]
    o_ref[...] = (r[:n] + r[n:]) + b_ref[...]


def _modulation(c, w_ada, b_ada):
    rows = c.shape[0]
    n = w_ada.shape[1]
    bn = 512
    return pl.pallas_call(
        _ada_kernel,
        grid=(n // bn,),
        in_specs=[pl.BlockSpec((rows, D_MODEL), lambda j: (0, 0)),
                  pl.BlockSpec((D_MODEL, bn), lambda j: (0, j)),
                  pl.BlockSpec((1, bn), lambda j: (0, j))],
        out_specs=pl.BlockSpec((rows, bn), lambda j: (0, j)),
        out_shape=jax.ShapeDtypeStruct((rows, n), F32),
        compiler_params=_params(),
        name="modulation",
    )(c, w_ada, b_ada.reshape(1, n))


def _norm_mod(x, g, scale, shift):
    ms = jnp.mean(x * x, axis=-1, keepdims=True)
    return ((x * lax.rsqrt(ms + EPS)) * g) * (1.0 + scale) + shift


def _inproj_kernel(x_ref, mod_ref, g_ref, w_ref, seg_ref, qg_ref, kg_ref,
                   qx_ref, k_ref, v_ref, u_ref):
    mod = mod_ref[0]
    h = _norm_mod(x_ref[...], g_ref[...], mod[1:2], mod[0:1]).astype(BF16)
    lane = lax.broadcasted_iota(jnp.int32, (TM_IN, LANES), 1)
    first_head = lane < HEAD_DIM

    def proj(c):
        return jnp.dot(h, w_ref[:, c * MXU_N:(c + 1) * MXU_N], preferred_element_type=F32)

    def head_norm(y, gain):
        ms = jnp.dot((y * y).astype(BF16), seg_ref[...], preferred_element_type=F32)
        return (y * lax.rsqrt(ms + EPS)) * gain

    n_chunks = ATT_WIDTH // MXU_N
    qk = [proj(c) for c in range(2 * n_chunks)]
    for j in range(2 * n_chunks):
        other = proj(2 * n_chunks + j).astype(BF16)
        if j < n_chunks:
            v_ref[:, j * MXU_N:(j + 1) * MXU_N] = other
            qn = head_norm(qk[j], qg_ref[:, j * MXU_N:(j + 1) * MXU_N])
            for pp in range(MXU_N // LANES):
                pair = qn[:, pp * LANES:(pp + 1) * LANES]
                base = (j * (MXU_N // LANES) + pp) * 2 * LANES
                qx_ref[:, base:base + LANES] = jnp.where(first_head, pair, 0.0).astype(BF16)
                qx_ref[:, base + LANES:base + 2 * LANES] = jnp.where(first_head, 0.0, pair).astype(BF16)
        else:
            c = j - n_chunks
            u_ref[:, c * MXU_N:(c + 1) * MXU_N] = other
            kn = head_norm(qk[j], kg_ref[:, c * MXU_N:(c + 1) * MXU_N])
            k_ref[:, c * MXU_N:(c + 1) * MXU_N] = kn.astype(BF16)


def _inproj(x2, mod3, boff, seq, g1, w_in, seg, qg, kg):
    n = x2.shape[0]
    tps = seq // TM_IN
    out_w = (2 * ATT_WIDTH, ATT_WIDTH, ATT_WIDTH, POOL_WIDTH)
    return pl.pallas_call(
        _inproj_kernel,
        grid=(n // TM_IN,),
        in_specs=[pl.BlockSpec((TM_IN, D_MODEL), lambda i: (i, 0)),
                  pl.BlockSpec((1, 6, D_MODEL), lambda i: (boff + i // tps, 0, 0)),
                  _const_spec((1, D_MODEL)),
                  _const_spec(w_in.shape),
                  _const_spec(seg.shape),
                  _const_spec((1, ATT_WIDTH)),
                  _const_spec((1, ATT_WIDTH))],
        out_specs=[pl.BlockSpec((TM_IN, w), lambda i: (i, 0)) for w in out_w],
        out_shape=[jax.ShapeDtypeStruct((n, w), BF16) for w in out_w],
        compiler_params=_params(),
        name="inproj",
    )(x2, mod3, g1, w_in, seg, qg, kg)


def _mix_kernel(q_ref, kp_ref, kc_ref, kn_ref, vp_ref, vc_ref, vn_ref,
                up_ref, uc_ref, un_ref, x_ref, mod_ref, bias_ref,
                wout_ref, o_ref,
                kbuf, vbuf, mixed, sbuf_a, sbuf_b, ubuf, lvl, *, tps, rows, seq):
    ti = lax.rem(pl.program_id(0), tps)
    r0 = ti * ROWS_PER_TILE

    kbuf[0:HALO_TOK, :] = kp_ref[...]
    kbuf[HALO_TOK:HALO_TOK + TM, :] = kc_ref[...]
    kbuf[HALO_TOK + TM:, :] = kn_ref[...]
    vbuf[0:HALO_TOK, :] = vp_ref[...]
    vbuf[HALO_TOK:HALO_TOK + TM, :] = vc_ref[...]
    vbuf[HALO_TOK + TM:, :] = vn_ref[...]

    lane = lax.broadcasted_iota(jnp.int32, (GRID_W, LANES), 1)
    n_keys = WIN_ROWS * GRID_W

    def window(lr):
        r = r0 + lr
        rs = jnp.clip(r - WIN_ROWS // 2, 0, rows - WIN_ROWS)
        off = pl.multiple_of((rs - r0 + HALO_ROWS) * GRID_W, GRID_W)
        return r - rs, off, pl.multiple_of(lr * GRID_W, GRID_W)

    def scores(win, dst, p):
        var, off, qoff = win
        qp = q_ref[pl.ds(qoff, GRID_W), p * 2 * LANES:(p + 1) * 2 * LANES]
        q2 = jnp.concatenate([qp[:, :LANES], qp[:, LANES:]], axis=0)
        kw = kbuf[pl.ds(off, n_keys), p * LANES:(p + 1) * LANES]
        s = lax.dot_general(q2, kw, (((1,), (1,)), ((), ())), preferred_element_type=F32)
        dst[p] = s + bias_ref[var, p]

    def attend(win, src, p):
        _, off, qoff = win
        s = src[p]
        m = jnp.max(s, axis=-1, keepdims=True)
        e = jnp.exp(s - m)
        l = jnp.sum(e, axis=-1, keepdims=True)
        vw = vbuf[pl.ds(off, n_keys), p * LANES:(p + 1) * LANES]
        o2 = jnp.dot(e.astype(BF16), vw, preferred_element_type=F32) / l
        o = jnp.where(lane < HEAD_DIM, o2[:GRID_W], o2[GRID_W:])
        mixed[pl.ds(qoff, GRID_W), p * LANES:(p + 1) * LANES] = o.astype(BF16)

    def row_step(lr, src, dst):
        cur, nxt = window(lr), window(lr + 1)
        scores(nxt, dst, 0)
        scores(nxt, dst, 1)
        for p in range(N_PAIRS):
            attend(cur, src, p)
            if p + 2 < N_PAIRS:
                scores(nxt, dst, p + 2)

    for p in range(N_PAIRS):
        scores(window(0), sbuf_a, p)

    def two_rows(j, carry):
        row_step(2 * j, sbuf_a, sbuf_b)
        row_step(2 * j + 1, sbuf_b, sbuf_a)
        return carry

    lax.fori_loop(0, ROWS_PER_TILE // 2 - 1, two_rows, 0)
    row_step(ROWS_PER_TILE - 2, sbuf_a, sbuf_b)
    for p in range(N_PAIRS):
        attend(window(ROWS_PER_TILE - 1), sbuf_b, p)

    ext = TM + 4 * SUB
    inner = TM + 2 * SUB
    centre = 2 * SUB
    ubuf[0:SUB, :] = jnp.zeros((SUB, POOL_WIDTH), F32)
    ubuf[ext - SUB:ext, :] = jnp.zeros((SUB, POOL_WIDTH), F32)
    ubuf[SUB:centre, :] = jnp.where(ti > 0, up_ref[...].astype(F32)[SUB:], 0.0)
    ubuf[centre:centre + TM, :] = uc_ref[...].astype(F32)
    ubuf[centre + TM:ext - SUB, :] = jnp.where(ti < tps - 1, un_ref[...].astype(F32)[:SUB], 0.0)
    lvl[:, :, 0:SUB, :] = jnp.zeros(lvl.shape[:2] + (SUB, POOL_GROUP), F32)
    lvl[:, :, ext - SUB:ext, :] = jnp.zeros(lvl.shape[:2] + (SUB, POOL_GROUP), F32)
    tpos = ti * TM + lax.broadcasted_iota(jnp.int32, (TM, POOL_GROUP), 0)
    for g, w in enumerate(POOL_WINDOWS):
        cols = slice(g * POOL_GROUP, (g + 1) * POOL_GROUP)
        if w == 2:
            acc = ubuf[pl.ds(centre - 1, TM), cols] + ubuf[pl.ds(centre, TM), cols]
        else:
            lv = lvl.at[g - 1]
            lv[0, SUB:SUB + inner, :] = (ubuf[pl.ds(SUB - 1, inner), cols]
                                         + ubuf[pl.ds(SUB, inner), cols])
            half, k = 1, 0
            while 4 * half < w:
                lv[1 - k, SUB:SUB + inner, :] = (lv[k, pl.ds(SUB - half, inner), :]
                                                 + lv[k, pl.ds(SUB + half, inner), :])
                half, k = 2 * half, 1 - k
            acc = lv[k, pl.ds(centre - half, TM), :] + lv[k, pl.ds(centre + half, TM), :]
        lo = jnp.maximum(tpos - w // 2, 0)
        hi = jnp.minimum(tpos - w // 2 + w, seq)
        cnt = (hi - lo).astype(F32)
        pooled = (acc / cnt - ubuf[centre:centre + TM, cols]).astype(BF16)
        mixed[:, ATT_WIDTH + g * POOL_GROUP:ATT_WIDTH + (g + 1) * POOL_GROUP] = pooled

    out = jnp.dot(mixed[...], wout_ref[...], preferred_element_type=F32)
    o_ref[...] = x_ref[...] + mod_ref[0][2:3] * out


def _fold_kernel(wp_ref, ps_ref, wo_ref, o_ref):
    w = wp_ref[0] * ps_ref[0]
    o_ref[...] = jnp.dot(w, wo_ref[...], preferred_element_type=F32,
                         precision=lax.Precision.HIGHEST)


def _fold_pool(w_pool, pool_scale, w_out):
    n_g = len(POOL_WINDOWS)
    first = ATT_WIDTH // POOL_GROUP
    return pl.pallas_call(
        _fold_kernel,
        grid=(n_g,),
        in_specs=[pl.BlockSpec((1, POOL_GROUP, POOL_GROUP), lambda g: (g, 0, 0)),
                  pl.BlockSpec((1, 1, POOL_GROUP), lambda g: (g, 0, 0)),
                  pl.BlockSpec((POOL_GROUP, D_MODEL), lambda g: (first + g, 0))],
        out_specs=pl.BlockSpec((POOL_GROUP, D_MODEL), lambda g: (g, 0)),
        out_shape=jax.ShapeDtypeStruct((POOL_WIDTH, D_MODEL), F32),
        compiler_params=_params(),
        name="fold_pool",
    )(w_pool, pool_scale.reshape(n_g, 1, POOL_GROUP), w_out)


def _mix(x2, qx, k, v, u, mod3, boff, seq, bias, w_out):
    n = x2.shape[0]
    tps = seq // TM
    rows = seq // GRID_W
    hb = TM // HALO_TOK
    n_hb = n // HALO_TOK
    ub = TM // SUB16
    n_ub = n // SUB16
    cur = lambda i: (i, 0)
    kprev = lambda i: (jnp.maximum(i * hb - 1, 0), 0)
    knext = lambda i: (jnp.minimum((i + 1) * hb, n_hb - 1), 0)
    uprev = lambda i: (jnp.maximum(i * ub - 1, 0), 0)
    unext = lambda i: (jnp.minimum((i + 1) * ub, n_ub - 1), 0)
    kernel = functools.partial(_mix_kernel, tps=tps, rows=rows, seq=seq)
    return pl.pallas_call(
        kernel,
        grid=(n // TM,),
        in_specs=[pl.BlockSpec((TM, 2 * ATT_WIDTH), cur),
                  pl.BlockSpec((HALO_TOK, ATT_WIDTH), kprev),
                  pl.BlockSpec((TM, ATT_WIDTH), cur),
                  pl.BlockSpec((HALO_TOK, ATT_WIDTH), knext),
                  pl.BlockSpec((HALO_TOK, ATT_WIDTH), kprev),
                  pl.BlockSpec((TM, ATT_WIDTH), cur),
                  pl.BlockSpec((HALO_TOK, ATT_WIDTH), knext),
                  pl.BlockSpec((SUB16, POOL_WIDTH), uprev),
                  pl.BlockSpec((TM, POOL_WIDTH), cur),
                  pl.BlockSpec((SUB16, POOL_WIDTH), unext),
                  pl.BlockSpec((TM, D_MODEL), cur),
                  pl.BlockSpec((1, 6, D_MODEL), lambda i: (boff + i // tps, 0, 0)),
                  _const_spec(bias.shape),
                  _const_spec(w_out.shape)],
        out_specs=pl.BlockSpec((TM, D_MODEL), cur),
        out_shape=jax.ShapeDtypeStruct((n, D_MODEL), F32),
        scratch_shapes=[pltpu.VMEM((TM + 2 * HALO_TOK, ATT_WIDTH), BF16),
                        pltpu.VMEM((TM + 2 * HALO_TOK, ATT_WIDTH), BF16),
                        pltpu.VMEM((TM, D_MODEL), BF16),
                        pltpu.VMEM((N_PAIRS, 2 * GRID_W, WIN_ROWS * GRID_W), F32),
                        pltpu.VMEM((N_PAIRS, 2 * GRID_W, WIN_ROWS * GRID_W), F32),
                        pltpu.VMEM((TM + 4 * SUB, POOL_WIDTH), F32),
                        pltpu.VMEM((len(POOL_WINDOWS) - 1, 2, TM + 4 * SUB, POOL_GROUP), F32)],
        compiler_params=_params(),
        name="mix",
    )(qx, k, k, k, v, v, v, u, u, u, x2, mod3, bias, w_out)


def _mlp_kernel(xp_ref, xc_ref, xn_ref, mod_ref, g_ref, wup_ref, cw_ref, cb_ref, wdown_ref,
                o_ref, act, *, tps, tm):
    ti = lax.rem(pl.program_id(0), tps)
    mod = mod_ref[0]
    shift, scale, gate = mod[3:4], mod[4:5], mod[5:6]
    g = g_ref[...]
    xc = xc_ref[...]
    hp = jnp.where(ti > 0, _norm_mod(xp_ref[...], g, scale, shift), 0.0)
    hn = jnp.where(ti < tps - 1, _norm_mod(xn_ref[...], g, scale, shift), 0.0)
    h = jnp.concatenate([hp, _norm_mod(xc, g, scale, shift), hn], axis=0).astype(BF16)
    ext = tm + 2 * SUB

    def conv(cols):
        up = jnp.dot(h, wup_ref[:, cols], preferred_element_type=F32)
        cw = cw_ref[:, cols]
        prev = pltpu.roll(up, 1, 0)[SUB:SUB + tm]
        nxt = pltpu.roll(up, ext - 1, 0)[SUB:SUB + tm]
        return prev * cw[0:1] + up[SUB:SUB + tm] * cw[1:2] + nxt * cw[2:3] + cb_ref[:, cols]

    for c in range(D_FF // FC):
        gt = conv(slice(c * FC, (c + 1) * FC))
        vl = conv(slice(D_FF + c * FC, D_FF + (c + 1) * FC))
        act[:, c * FC:(c + 1) * FC] = ((gt * jax.nn.sigmoid(gt)) * vl).astype(BF16)

    y = jnp.dot(act[...], wdown_ref[...], preferred_element_type=F32)
    o_ref[...] = xc + gate * y


def _mlp(x1, mod3, boff, seq, g2, w_up, conv_w, conv_b, w_down):
    n = x1.shape[0]
    tm = TM_MLP
    tps = seq // tm
    hb = tm // SUB
    n_hb = n // SUB
    kernel = functools.partial(_mlp_kernel, tps=tps, tm=tm)
    return pl.pallas_call(
        kernel,
        grid=(n // tm,),
        in_specs=[pl.BlockSpec((SUB, D_MODEL), lambda i: (jnp.maximum(i * hb - 1, 0), 0)),
                  pl.BlockSpec((tm, D_MODEL), lambda i: (i, 0)),
                  pl.BlockSpec((SUB, D_MODEL), lambda i: (jnp.minimum((i + 1) * hb, n_hb - 1), 0)),
                  pl.BlockSpec((1, 6, D_MODEL), lambda i: (boff + i // tps, 0, 0)),
                  _const_spec((1, D_MODEL)),
                  _const_spec(w_up.shape),
                  _const_spec(conv_w.shape),
                  _const_spec((1, 2 * D_FF)),
                  _const_spec(w_down.shape)],
        out_specs=pl.BlockSpec((tm, D_MODEL), lambda i: (i, 0)),
        out_shape=jax.ShapeDtypeStruct((n, D_MODEL), F32),
        scratch_shapes=[pltpu.VMEM((tm, D_FF), BF16)],
        compiler_params=_params(),
        name="mlp",
    )(x1, x1, x1, mod3, g2, w_up, conv_w, conv_b, w_down)


def _bias_table(rpb):
    cols = np.arange(GRID_W)
    start = np.clip(cols - WIN_COLS // 2, 0, GRID_W - WIN_COLS)
    onehot = np.zeros((2 * WIN_COLS - 1, GRID_W, GRID_W), np.float32)
    inside = np.zeros((GRID_W, GRID_W), bool)
    for c in cols:
        for j in range(WIN_COLS):
            c2 = start[c] + j
            onehot[c2 - c + WIN_COLS - 1, c, c2] = 1.0
            inside[c, c2] = True
    t = jnp.einsum('hrd,dcx->hrcx', rpb.astype(F32), jnp.asarray(onehot),
                   precision=lax.Precision.HIGHEST)
    t = jnp.where(jnp.asarray(inside), t, NEG)
    tab = jnp.stack([t[:, WIN_ROWS - 1 - v:2 * WIN_ROWS - 1 - v] for v in range(WIN_ROWS)])
    tab = tab.transpose(0, 1, 3, 2, 4)
    return tab.reshape(WIN_ROWS, N_PAIRS, 2 * GRID_W, WIN_ROWS * GRID_W)


def _layer(xs, mod3, p):
    (norm1_g, norm2_g, w_in, q_norm_g, k_norm_g, rpb, w_pool, pool_scale, w_out,
     w_up, conv_w, conv_b, w_down) = p
    g1 = norm1_g.reshape(1, D_MODEL)
    g2 = norm2_g.reshape(1, D_MODEL)
    w_in_b = w_in.astype(BF16)
    heads = np.arange(MXU_N) // HEAD_DIM
    seg = jnp.asarray((heads[:, None] == heads[None, :]).astype(np.float32) / HEAD_DIM, BF16)
    qg = (jnp.tile(q_norm_g, N_HEADS) * HEAD_DIM ** -0.5).reshape(1, ATT_WIDTH)
    kg = jnp.tile(k_norm_g, N_HEADS).reshape(1, ATT_WIDTH)
    bias = _bias_table(rpb)
    w_out_b = jnp.concatenate([w_out[:ATT_WIDTH], _fold_pool(w_pool, pool_scale, w_out)],
                              axis=0).astype(BF16)
    w_up_b = w_up.astype(BF16)
    cb = conv_b.reshape(1, 2 * D_FF)
    w_down_b = w_down.astype(BF16)

    outs = []
    boff = 0
    for x in xs:
        b, seq, _ = x.shape
        x2 = x.reshape(b * seq, D_MODEL)
        qx, k, v, u = _inproj(x2, mod3, boff, seq, g1, w_in_b, seg, qg, kg)
        x1 = _mix(x2, qx, k, v, u, mod3, boff, seq, bias, w_out_b)
        y = _mlp(x1, mod3, boff, seq, g2, w_up_b, conv_w, cb, w_down_b)
        outs.append(y.reshape(b, seq, D_MODEL))
        boff += b
    return outs


def kernel(x_prompt, x_sample, c_prompt, c_sample, w_ada, b_ada, norm1_g, norm2_g, w_in,
           q_norm_g, k_norm_g, rpb, w_pool, pool_scale, w_out, w_up, conv_w, conv_b, w_down):
    xs = [x_prompt, x_sample]
    c = jnp.concatenate([c_prompt, c_sample], axis=0)
    n_b = c.shape[0]
    c = jnp.pad(c, ((0, -n_b % SUB16), (0, 0)))
    for l in range(w_ada.shape[0]):
        mod = _modulation(c, w_ada[l], b_ada[l])
        mod3 = mod.reshape(c.shape[0], 6, D_MODEL)
        p = (norm1_g[l], norm2_g[l], w_in[l], q_norm_g[l], k_norm_g[l], rpb[l], w_pool[l],
             pool_scale[l], w_out[l], w_up[l], conv_w[l], conv_b[l], w_down[l])
        xs = _layer(xs, mod3, p)
    return (xs[0], xs[1])
```

```python
import functools

import numpy as np
import jax
import jax.numpy as jnp
from jax import lax
from jax.experimental import pallas as pl
from jax.experimental.pallas import tpu as pltpu

F32 = jnp.float32
BF16 = jnp.bfloat16

D_MODEL = 1024
GRID_W = 64
HEAD_DIM = 64
N_HEADS = 8
N_PAIRS = N_HEADS // 2
ATT_WIDTH = N_HEADS * HEAD_DIM
POOL_WINDOWS = (2, 4, 8, 16)
POOL_GROUP = 128
POOL_WIDTH = POOL_GROUP * len(POOL_WINDOWS)
WIN_ROWS = 8
WIN_COLS = 16
D_FF = 2816
EPS = 1e-6
NEG = -1e30

LANES = 128
MXU_N = 256
TM = 1024
ROWS_PER_TILE = TM // GRID_W
TM_MLP = 1024
TM_IN = 1024
HALO_ROWS = 4
HALO_TOK = HALO_ROWS * GRID_W
SUB = 8
SUB16 = 16
FC = 512
VMEM_LIMIT = 56 * 1024 * 1024


def _const_spec(shape):
    nd = len(shape)
    return pl.BlockSpec(shape, lambda i: (0,) * nd, pipeline_mode=pl.Buffered(1))


def _params(vmem=VMEM_LIMIT):
    return pltpu.CompilerParams(dimension_semantics=("arbitrary",), vmem_limit_bytes=vmem)


def _ada_kernel(c_ref, w_ref, b_ref, o_ref):
    c = c_ref[...]
    s = c * jax.nn.sigmoid(c)
    hi = s.astype(BF16)
    lo = (s - hi.astype(F32)).astype(BF16)
    r = jnp.dot(jnp.concatenate([hi, lo], axis=0), w_ref[...].astype(BF16),
                preferred_element_type=F32)
    n = c.shape[0]
    o_ref[...] = (r[:n] + r[n:]) + b_ref[...]


def _modulation(c, w_ada, b_ada):
    rows = c.shape[0]
    n = w_ada.shape[1]
    bn = 1536
    return pl.pallas_call(
        _ada_kernel,
        grid=(n // bn,),
        in_specs=[pl.BlockSpec((rows, D_MODEL), lambda j: (0, 0)),
                  pl.BlockSpec((D_MODEL, bn), lambda j: (0, j)),
                  pl.BlockSpec((1, bn), lambda j: (0, j))],
        out_specs=pl.BlockSpec((rows, bn), lambda j: (0, j)),
        out_shape=jax.ShapeDtypeStruct((rows, n), F32),
        compiler_params=_params(),
        name="modulation",
    )(c, w_ada, b_ada.reshape(1, n))


def _norm_mod(x, g, scale, shift):
    ms = jnp.mean(x * x, axis=-1, keepdims=True)
    return ((x * lax.rsqrt(ms + EPS)) * g) * (1.0 + scale) + shift


def _inproj_kernel(x_ref, mod_ref, g_ref, w_ref, seg_ref, qg_ref, kg_ref,
                   qx_ref, k_ref, v_ref, u_ref):
    mod = mod_ref[0]
    h = _norm_mod(x_ref[...], g_ref[...], mod[1:2], mod[0:1]).astype(BF16)
    lane = lax.broadcasted_iota(jnp.int32, (TM_IN, LANES), 1)
    first_head = lane < HEAD_DIM

    def proj(c):
        return jnp.dot(h, w_ref[:, c * MXU_N:(c + 1) * MXU_N], preferred_element_type=F32)

    def head_norm(y, gain):
        ms = jnp.dot((y * y).astype(BF16), seg_ref[...], preferred_element_type=F32)
        return (y * lax.rsqrt(ms + EPS)) * gain

    n_chunks = ATT_WIDTH // MXU_N
    qk = [proj(c) for c in range(2 * n_chunks)]
    for j in range(2 * n_chunks):
        other = proj(2 * n_chunks + j).astype(BF16)
        if j < n_chunks:
            v_ref[:, j * MXU_N:(j + 1) * MXU_N] = other
            qn = head_norm(qk[j], qg_ref[:, j * MXU_N:(j + 1) * MXU_N])
            for pp in range(MXU_N // LANES):
                pair = qn[:, pp * LANES:(pp + 1) * LANES]
                base = (j * (MXU_N // LANES) + pp) * 2 * LANES
                qx_ref[:, base:base + LANES] = jnp.where(first_head, pair, 0.0).astype(BF16)
                qx_ref[:, base + LANES:base + 2 * LANES] = jnp.where(first_head, 0.0, pair).astype(BF16)
        else:
            c = j - n_chunks
            u_ref[:, c * MXU_N:(c + 1) * MXU_N] = other
            kn = head_norm(qk[j], kg_ref[:, c * MXU_N:(c + 1) * MXU_N])
            k_ref[:, c * MXU_N:(c + 1) * MXU_N] = kn.astype(BF16)


def _inproj(x2, mod3, boff, seq, g1, w_in, seg, qg, kg):
    n = x2.shape[0]
    tps = seq // TM_IN
    out_w = (2 * ATT_WIDTH, ATT_WIDTH, ATT_WIDTH, POOL_WIDTH)
    return pl.pallas_call(
        _inproj_kernel,
        grid=(n // TM_IN,),
        in_specs=[pl.BlockSpec((TM_IN, D_MODEL), lambda i: (i, 0)),
                  pl.BlockSpec((1, 6, D_MODEL), lambda i: (boff + i // tps, 0, 0)),
                  _const_spec((1, D_MODEL)),
                  _const_spec(w_in.shape),
                  _const_spec(seg.shape),
                  _const_spec((1, ATT_WIDTH)),
                  _const_spec((1, ATT_WIDTH))],
        out_specs=[pl.BlockSpec((TM_IN, w), lambda i: (i, 0)) for w in out_w],
        out_shape=[jax.ShapeDtypeStruct((n, w), BF16) for w in out_w],
        compiler_params=_params(),
        name="inproj",
    )(x2, mod3, g1, w_in, seg, qg, kg)


def _mix_kernel(q_ref, kp_ref, kc_ref, kn_ref, vp_ref, vc_ref, vn_ref,
                up_ref, uc_ref, un_ref, x_ref, mod_ref, bias_ref,
                wout_ref, o_ref,
                kbuf, vbuf, mixed, sbuf_a, sbuf_b, ubuf, lvl, *, tps, rows, seq):
    ti = lax.rem(pl.program_id(0), tps)
    r0 = ti * ROWS_PER_TILE

    kbuf[0:HALO_TOK, :] = kp_ref[...]
    kbuf[HALO_TOK:HALO_TOK + TM, :] = kc_ref[...]
    kbuf[HALO_TOK + TM:, :] = kn_ref[...]
    vbuf[0:HALO_TOK, :] = vp_ref[...]
    vbuf[HALO_TOK:HALO_TOK + TM, :] = vc_ref[...]
    vbuf[HALO_TOK + TM:, :] = vn_ref[...]

    lane = lax.broadcasted_iota(jnp.int32, (GRID_W, LANES), 1)
    n_keys = WIN_ROWS * GRID_W

    def window(lr):
        r = r0 + lr
        rs = jnp.clip(r - WIN_ROWS // 2, 0, rows - WIN_ROWS)
        off = pl.multiple_of((rs - r0 + HALO_ROWS) * GRID_W, GRID_W)
        return r - rs, off, pl.multiple_of(lr * GRID_W, GRID_W)

    def scores(win, dst, p):
        var, off, qoff = win
        qp = q_ref[pl.ds(qoff, GRID_W), p * 2 * LANES:(p + 1) * 2 * LANES]
        q2 = jnp.concatenate([qp[:, :LANES], qp[:, LANES:]], axis=0)
        kw = kbuf[pl.ds(off, n_keys), p * LANES:(p + 1) * LANES]
        s = lax.dot_general(q2, kw, (((1,), (1,)), ((), ())), preferred_element_type=F32)
        bias = jnp.concatenate([bias_ref[p, WIN_ROWS - 1 - var + 2 * j]
                                for j in range(WIN_ROWS // 2)], axis=-1)
        dst[p] = s + bias

    def attend(win, src, p):
        _, off, qoff = win
        s = src[p]
        m = jnp.max(s, axis=-1, keepdims=True)
        e = jnp.exp(s - m)
        l = jnp.sum(e, axis=-1, keepdims=True)
        vw = vbuf[pl.ds(off, n_keys), p * LANES:(p + 1) * LANES]
        o2 = jnp.dot(e.astype(BF16), vw, preferred_element_type=F32) / l
        o = jnp.where(lane < HEAD_DIM, o2[:GRID_W], o2[GRID_W:])
        mixed[pl.ds(qoff, GRID_W), p * LANES:(p + 1) * LANES] = o.astype(BF16)

    def row_step(lr, src, dst):
        cur, nxt = window(lr), window(lr + 1)
        scores(nxt, dst, 0)
        scores(nxt, dst, 1)
        for p in range(N_PAIRS):
            attend(cur, src, p)
            if p + 2 < N_PAIRS:
                scores(nxt, dst, p + 2)

    for p in range(N_PAIRS):
        scores(window(0), sbuf_a, p)

    def two_rows(j, carry):
        row_step(2 * j, sbuf_a, sbuf_b)
        row_step(2 * j + 1, sbuf_b, sbuf_a)
        return carry

    lax.fori_loop(0, ROWS_PER_TILE // 2 - 1, two_rows, 0)
    row_step(ROWS_PER_TILE - 2, sbuf_a, sbuf_b)
    for p in range(N_PAIRS):
        attend(window(ROWS_PER_TILE - 1), sbuf_b, p)

    ext = TM + 4 * SUB
    inner = TM + 2 * SUB
    centre = 2 * SUB
    ubuf[0:SUB, :] = jnp.zeros((SUB, POOL_WIDTH), F32)
    ubuf[ext - SUB:ext, :] = jnp.zeros((SUB, POOL_WIDTH), F32)
    ubuf[SUB:centre, :] = jnp.where(ti > 0, up_ref[...].astype(F32)[SUB:], 0.0)
    ubuf[centre:centre + TM, :] = uc_ref[...].astype(F32)
    ubuf[centre + TM:ext - SUB, :] = jnp.where(ti < tps - 1, un_ref[...].astype(F32)[:SUB], 0.0)
    lvl[:, :, 0:SUB, :] = jnp.zeros(lvl.shape[:2] + (SUB, POOL_GROUP), F32)
    lvl[:, :, ext - SUB:ext, :] = jnp.zeros(lvl.shape[:2] + (SUB, POOL_GROUP), F32)
    tpos = ti * TM + lax.broadcasted_iota(jnp.int32, (TM, POOL_GROUP), 0)
    for g, w in enumerate(POOL_WINDOWS):
        cols = slice(g * POOL_GROUP, (g + 1) * POOL_GROUP)
        if w == 2:
            acc = ubuf[pl.ds(centre - 1, TM), cols] + ubuf[pl.ds(centre, TM), cols]
        else:
            lv = lvl.at[g - 1]
            lv[0, SUB:SUB + inner, :] = (ubuf[pl.ds(SUB - 1, inner), cols]
                                         + ubuf[pl.ds(SUB, inner), cols])
            half, k = 1, 0
            while 4 * half < w:
                lv[1 - k, SUB:SUB + inner, :] = (lv[k, pl.ds(SUB - half, inner), :]
                                                 + lv[k, pl.ds(SUB + half, inner), :])
                half, k = 2 * half, 1 - k
            acc = lv[k, pl.ds(centre - half, TM), :] + lv[k, pl.ds(centre + half, TM), :]
        lo = jnp.maximum(tpos - w // 2, 0)
        hi = jnp.minimum(tpos - w // 2 + w, seq)
        cnt = (hi - lo).astype(F32)
        pooled = (acc / cnt - ubuf[centre:centre + TM, cols]).astype(BF16)
        mixed[:, ATT_WIDTH + g * POOL_GROUP:ATT_WIDTH + (g + 1) * POOL_GROUP] = pooled

    out = jnp.dot(mixed[...], wout_ref[...], preferred_element_type=F32)
    o_ref[...] = x_ref[...] + mod_ref[0][2:3] * out


def _fold_kernel(wp_ref, ps_ref, wo_ref, o_ref):
    w = wp_ref[0] * ps_ref[0]
    o_ref[...] = jnp.dot(w, wo_ref[...], preferred_element_type=F32,
                         precision=lax.Precision.HIGHEST)


def _fold_pool(w_pool, pool_scale, w_out):
    n_g = len(POOL_WINDOWS)
    first = ATT_WIDTH // POOL_GROUP
    return pl.pallas_call(
        _fold_kernel,
        grid=(n_g,),
        in_specs=[pl.BlockSpec((1, POOL_GROUP, POOL_GROUP), lambda g: (g, 0, 0)),
                  pl.BlockSpec((1, 1, POOL_GROUP), lambda g: (g, 0, 0)),
                  pl.BlockSpec((POOL_GROUP, D_MODEL), lambda g: (first + g, 0))],
        out_specs=pl.BlockSpec((POOL_GROUP, D_MODEL), lambda g: (g, 0)),
        out_shape=jax.ShapeDtypeStruct((POOL_WIDTH, D_MODEL), F32),
        compiler_params=_params(),
        name="fold_pool",
    )(w_pool, pool_scale.reshape(n_g, 1, POOL_GROUP), w_out)


def _mix(x2, qx, k, v, u, mod3, boff, seq, bias, w_out):
    n = x2.shape[0]
    tps = seq // TM
    rows = seq // GRID_W
    hb = TM // HALO_TOK
    n_hb = n // HALO_TOK
    ub = TM // SUB16
    n_ub = n // SUB16
    cur = lambda i: (i, 0)
    kprev = lambda i: (jnp.maximum(i * hb - 1, 0), 0)
    knext = lambda i: (jnp.minimum((i + 1) * hb, n_hb - 1), 0)
    uprev = lambda i: (jnp.maximum(i * ub - 1, 0), 0)
    unext = lambda i: (jnp.minimum((i + 1) * ub, n_ub - 1), 0)
    kernel = functools.partial(_mix_kernel, tps=tps, rows=rows, seq=seq)
    return pl.pallas_call(
        kernel,
        grid=(n // TM,),
        in_specs=[pl.BlockSpec((TM, 2 * ATT_WIDTH), cur),
                  pl.BlockSpec((HALO_TOK, ATT_WIDTH), kprev),
                  pl.BlockSpec((TM, ATT_WIDTH), cur),
                  pl.BlockSpec((HALO_TOK, ATT_WIDTH), knext),
                  pl.BlockSpec((HALO_TOK, ATT_WIDTH), kprev),
                  pl.BlockSpec((TM, ATT_WIDTH), cur),
                  pl.BlockSpec((HALO_TOK, ATT_WIDTH), knext),
                  pl.BlockSpec((SUB16, POOL_WIDTH), uprev),
                  pl.BlockSpec((TM, POOL_WIDTH), cur),
                  pl.BlockSpec((SUB16, POOL_WIDTH), unext),
                  pl.BlockSpec((TM, D_MODEL), cur),
                  pl.BlockSpec((1, 6, D_MODEL), lambda i: (boff + i // tps, 0, 0)),
                  _const_spec(bias.shape),
                  _const_spec(w_out.shape)],
        out_specs=pl.BlockSpec((TM, D_MODEL), cur),
        out_shape=jax.ShapeDtypeStruct((n, D_MODEL), F32),
        scratch_shapes=[pltpu.VMEM((TM + 2 * HALO_TOK, ATT_WIDTH), BF16),
                        pltpu.VMEM((TM + 2 * HALO_TOK, ATT_WIDTH), BF16),
                        pltpu.VMEM((TM, D_MODEL), BF16),
                        pltpu.VMEM((N_PAIRS, 2 * GRID_W, WIN_ROWS * GRID_W), F32),
                        pltpu.VMEM((N_PAIRS, 2 * GRID_W, WIN_ROWS * GRID_W), F32),
                        pltpu.VMEM((TM + 4 * SUB, POOL_WIDTH), F32),
                        pltpu.VMEM((len(POOL_WINDOWS) - 1, 2, TM + 4 * SUB, POOL_GROUP), F32)],
        compiler_params=_params(),
        name="mix",
    )(qx, k, k, k, v, v, v, u, u, u, x2, mod3, bias, w_out)


def _mlp_kernel(xp_ref, xc_ref, xn_ref, mod_ref, g_ref, wup_ref, cw_ref, cb_ref, wdown_ref,
                o_ref, act, *, tps, tm):
    ti = lax.rem(pl.program_id(0), tps)
    mod = mod_ref[0]
    shift, scale, gate = mod[3:4], mod[4:5], mod[5:6]
    g = g_ref[...]
    xc = xc_ref[...]
    hp = jnp.where(ti > 0, _norm_mod(xp_ref[...], g, scale, shift), 0.0)
    hn = jnp.where(ti < tps - 1, _norm_mod(xn_ref[...], g, scale, shift), 0.0)
    h = jnp.concatenate([hp, _norm_mod(xc, g, scale, shift), hn], axis=0).astype(BF16)
    ext = tm + 2 * SUB

    def conv(cols):
        up = jnp.dot(h, wup_ref[:, cols], preferred_element_type=F32)
        cw = cw_ref[:, cols]
        prev = pltpu.roll(up, 1, 0)[SUB:SUB + tm]
        nxt = pltpu.roll(up, ext - 1, 0)[SUB:SUB + tm]
        return prev * cw[0:1] + up[SUB:SUB + tm] * cw[1:2] + nxt * cw[2:3] + cb_ref[:, cols]

    for lo in range(0, D_FF, FC):
        hi = min(lo + FC, D_FF)
        gt = conv(slice(lo, hi))
        vl = conv(slice(D_FF + lo, D_FF + hi))
        act[:, lo:hi] = ((gt * jax.nn.sigmoid(gt)) * vl).astype(BF16)

    y = jnp.dot(act[...], wdown_ref[...], preferred_element_type=F32)
    o_ref[...] = xc + gate * y


def _mlp(x1, mod3, boff, seq, g2, w_up, conv_w, conv_b, w_down):
    n = x1.shape[0]
    tm = TM_MLP
    tps = seq // tm
    hb = tm // SUB
    n_hb = n // SUB
    kernel = functools.partial(_mlp_kernel, tps=tps, tm=tm)
    return pl.pallas_call(
        kernel,
        grid=(n // tm,),
        in_specs=[pl.BlockSpec((SUB, D_MODEL), lambda i: (jnp.maximum(i * hb - 1, 0), 0)),
                  pl.BlockSpec((tm, D_MODEL), lambda i: (i, 0)),
                  pl.BlockSpec((SUB, D_MODEL), lambda i: (jnp.minimum((i + 1) * hb, n_hb - 1), 0)),
                  pl.BlockSpec((1, 6, D_MODEL), lambda i: (boff + i // tps, 0, 0)),
                  _const_spec((1, D_MODEL)),
                  _const_spec(w_up.shape),
                  _const_spec(conv_w.shape),
                  _const_spec((1, 2 * D_FF)),
                  _const_spec(w_down.shape)],
        out_specs=pl.BlockSpec((tm, D_MODEL), lambda i: (i, 0)),
        out_shape=jax.ShapeDtypeStruct((n, D_MODEL), F32),
        scratch_shapes=[pltpu.VMEM((tm, D_FF), BF16)],
        compiler_params=_params(),
        name="mlp",
    )(x1, x1, x1, mod3, g2, w_up, conv_w, conv_b, w_down)


def _bias_table(rpb):
    cols = np.arange(GRID_W)
    start = np.clip(cols - WIN_COLS // 2, 0, GRID_W - WIN_COLS)
    onehot = np.zeros((2 * WIN_COLS - 1, GRID_W, GRID_W), np.float32)
    inside = np.zeros((GRID_W, GRID_W), bool)
    for c in cols:
        for j in range(WIN_COLS):
            c2 = start[c] + j
            onehot[c2 - c + WIN_COLS - 1, c, c2] = 1.0
            inside[c, c2] = True
    t = jnp.einsum('hrd,dcx->hrcx', rpb.astype(F32), jnp.asarray(onehot),
                   precision=lax.Precision.HIGHEST)
    t = jnp.where(jnp.asarray(inside), t, NEG)
    two = jnp.concatenate([t[:, :-1], t[:, 1:]], axis=-1)
    two = two.reshape(N_PAIRS, 2, 2 * WIN_ROWS - 2, GRID_W, 2 * GRID_W).transpose(0, 2, 1, 3, 4)
    return two.reshape(N_PAIRS, 2 * WIN_ROWS - 2, 2 * GRID_W, 2 * GRID_W)


def _layer(xs, mod3, p):
    (norm1_g, norm2_g, w_in, q_norm_g, k_norm_g, rpb, w_pool, pool_scale, w_out,
     w_up, conv_w, conv_b, w_down) = p
    g1 = norm1_g.reshape(1, D_MODEL)
    g2 = norm2_g.reshape(1, D_MODEL)
    w_in_b = w_in.astype(BF16)
    heads = np.arange(MXU_N) // HEAD_DIM
    seg = jnp.asarray((heads[:, None] == heads[None, :]).astype(np.float32) / HEAD_DIM, BF16)
    qg = (jnp.tile(q_norm_g, N_HEADS) * HEAD_DIM ** -0.5).reshape(1, ATT_WIDTH)
    kg = jnp.tile(k_norm_g, N_HEADS).reshape(1, ATT_WIDTH)
    bias = _bias_table(rpb)
    w_out_b = jnp.concatenate([w_out[:ATT_WIDTH], _fold_pool(w_pool, pool_scale, w_out)],
                              axis=0).astype(BF16)
    w_up_b = w_up.astype(BF16)
    cb = conv_b.reshape(1, 2 * D_FF)
    w_down_b = w_down.astype(BF16)

    outs = []
    boff = 0
    for x in xs:
        b, seq, _ = x.shape
        x2 = x.reshape(b * seq, D_MODEL)
        qx, k, v, u = _inproj(x2, mod3, boff, seq, g1, w_in_b, seg, qg, kg)
        x1 = _mix(x2, qx, k, v, u, mod3, boff, seq, bias, w_out_b)
        y = _mlp(x1, mod3, boff, seq, g2, w_up_b, conv_w, cb, w_down_b)
        outs.append(y.reshape(b, seq, D_MODEL))
        boff += b
    return outs


def kernel(x_prompt, x_sample, c_prompt, c_sample, w_ada, b_ada, norm1_g, norm2_g, w_in,
           q_norm_g, k_norm_g, rpb, w_pool, pool_scale, w_out, w_up, conv_w, conv_b, w_down):
    xs = [x_prompt, x_sample]
    c = jnp.concatenate([c_prompt, c_sample], axis=0)
    n_b = c.shape[0]
    c = jnp.pad(c, ((0, -n_b % SUB16), (0, 0)))
    for l in range(w_ada.shape[0]):
        mod = _modulation(c, w_ada[l], b_ada[l])
        mod3 = mod.reshape(c.shape[0], 6, D_MODEL)
        p = (norm1_g[l], norm2_g[l], w_in[l], q_norm_g[l], k_norm_g[l], rpb[l], w_pool[l],
             pool_scale[l], w_out[l], w_up[l], conv_w[l], conv_b[l], w_down[l])
        xs = _layer(xs, mod3, p)
    return (xs[0], xs[1])
```

```python
import functools

import numpy as np
import jax
import jax.numpy as jnp
from jax import lax
from jax.experimental import pallas as pl
from jax.experimental.pallas import tpu as pltpu

F32 = jnp.float32
BF16 = jnp.bfloat16

D_MODEL = 1024
GRID_W = 64
HEAD_DIM = 64
N_HEADS = 8
N_PAIRS = N_HEADS // 2
ATT_WIDTH = N_HEADS * HEAD_DIM
POOL_WINDOWS = (2, 4, 8, 16)
POOL_GROUP = 128
POOL_WIDTH = POOL_GROUP * len(POOL_WINDOWS)
WIN_ROWS = 8
WIN_COLS = 16
D_FF = 2816
EPS = 1e-6
NEG = -1e30
LOG2E = 1.4426950408889634

LANES = 128
MXU_N = 256
TM = 1024
ROWS_PER_TILE = TM // GRID_W
ROW_UNROLL = 4
TM_MLP = 1024
TM_IN = 1024
HALO_ROWS = 4
HALO_TOK = HALO_ROWS * GRID_W
SUB = 8
SUB16 = 16
FC = 512
VMEM_LIMIT = 56 * 1024 * 1024


def _const_spec(shape):
    nd = len(shape)
    return pl.BlockSpec(shape, lambda i: (0,) * nd, pipeline_mode=pl.Buffered(1))


def _params(vmem=VMEM_LIMIT):
    return pltpu.CompilerParams(dimension_semantics=("arbitrary",), vmem_limit_bytes=vmem)


def _ada_kernel(c_ref, w_ref, b_ref, o_ref):
    c = c_ref[...]
    s = c * jax.nn.sigmoid(c)
    hi = s.astype(BF16)
    lo = (s - hi.astype(F32)).astype(BF16)
    r = jnp.dot(jnp.concatenate([hi, lo], axis=0), w_ref[...].astype(BF16),
                preferred_element_type=F32)
    n = c.shape[0]
    o_ref[...] = (r[:n] + r[n:]) + b_ref[...]


def _modulation(c, w_ada, b_ada):
    rows = c.shape[0]
    n = w_ada.shape[1]
    bn = 1536
    return pl.pallas_call(
        _ada_kernel,
        grid=(n // bn,),
        in_specs=[pl.BlockSpec((rows, D_MODEL), lambda j: (0, 0)),
                  pl.BlockSpec((D_MODEL, bn), lambda j: (0, j)),
                  pl.BlockSpec((1, bn), lambda j: (0, j))],
        out_specs=pl.BlockSpec((rows, bn), lambda j: (0, j)),
        out_shape=jax.ShapeDtypeStruct((rows, n), F32),
        compiler_params=_params(),
        name="modulation",
    )(c, w_ada, b_ada.reshape(1, n))


def _norm_mod(x, g, scale, shift):
    ms = jnp.mean(x * x, axis=-1, keepdims=True)
    return ((x * lax.rsqrt(ms + EPS)) * g) * (1.0 + scale) + shift


def _inproj_kernel(x_ref, mod_ref, g_ref, w_ref, seg_ref, qg_ref, kg_ref,
                   qx_ref, k_ref, v_ref, u_ref):
    mod = mod_ref[0]
    h = _norm_mod(x_ref[...], g_ref[...], mod[1:2], mod[0:1]).astype(BF16)
    lane = lax.broadcasted_iota(jnp.int32, (TM_IN, LANES), 1)
    first_head = lane < HEAD_DIM

    def proj(c):
        return jnp.dot(h, w_ref[:, c * MXU_N:(c + 1) * MXU_N], preferred_element_type=F32)

    def head_norm(y, gain):
        ms = jnp.dot((y * y).astype(BF16), seg_ref[...], preferred_element_type=F32)
        return (y * lax.rsqrt(ms + EPS)) * gain

    n_chunks = ATT_WIDTH // MXU_N
    qk = [proj(c) for c in range(2 * n_chunks)]
    for j in range(2 * n_chunks):
        other = proj(2 * n_chunks + j).astype(BF16)
        if j < n_chunks:
            v_ref[:, j * MXU_N:(j + 1) * MXU_N] = other
            qn = head_norm(qk[j], qg_ref[:, j * MXU_N:(j + 1) * MXU_N])
            for pp in range(MXU_N // LANES):
                pair = qn[:, pp * LANES:(pp + 1) * LANES]
                base = (j * (MXU_N // LANES) + pp) * 2 * LANES
                qx_ref[:, base:base + LANES] = jnp.where(first_head, pair, 0.0).astype(BF16)
                qx_ref[:, base + LANES:base + 2 * LANES] = jnp.where(first_head, 0.0, pair).astype(BF16)
        else:
            c = j - n_chunks
            u_ref[:, c * MXU_N:(c + 1) * MXU_N] = other
            kn = head_norm(qk[j], kg_ref[:, c * MXU_N:(c + 1) * MXU_N])
            k_ref[:, c * MXU_N:(c + 1) * MXU_N] = kn.astype(BF16)


def _inproj(x2, mod3, boff, seq, g1, w_in, seg, qg, kg):
    n = x2.shape[0]
    tps = seq // TM_IN
    out_w = (2 * ATT_WIDTH, ATT_WIDTH, ATT_WIDTH, POOL_WIDTH)
    return pl.pallas_call(
        _inproj_kernel,
        grid=(n // TM_IN,),
        in_specs=[pl.BlockSpec((TM_IN, D_MODEL), lambda i: (i, 0)),
                  pl.BlockSpec((1, 6, D_MODEL), lambda i: (boff + i // tps, 0, 0)),
                  _const_spec((1, D_MODEL)),
                  _const_spec(w_in.shape),
                  _const_spec(seg.shape),
                  _const_spec((1, ATT_WIDTH)),
                  _const_spec((1, ATT_WIDTH))],
        out_specs=[pl.BlockSpec((TM_IN, w), lambda i: (i, 0)) for w in out_w],
        out_shape=[jax.ShapeDtypeStruct((n, w), BF16) for w in out_w],
        compiler_params=_params(),
        name="inproj",
    )(x2, mod3, g1, w_in, seg, qg, kg)


def _mix_kernel(q_ref, kp_ref, kc_ref, kn_ref, vp_ref, vc_ref, vn_ref,
                up_ref, uc_ref, un_ref, x_ref, mod_ref, bias_ref,
                wout_ref, o_ref,
                kbuf, vbuf, mixed, sbuf_a, sbuf_b, ubuf, lvl, *, tps, rows, seq):
    ti = lax.rem(pl.program_id(0), tps)
    r0 = ti * ROWS_PER_TILE

    kbuf[0:HALO_TOK, :] = kp_ref[...]
    kbuf[HALO_TOK:HALO_TOK + TM, :] = kc_ref[...]
    kbuf[HALO_TOK + TM:, :] = kn_ref[...]
    vbuf[0:HALO_TOK, :] = vp_ref[...]
    vbuf[HALO_TOK:HALO_TOK + TM, :] = vc_ref[...]
    vbuf[HALO_TOK + TM:, :] = vn_ref[...]

    lane = lax.broadcasted_iota(jnp.int32, (GRID_W, LANES), 1)
    n_keys = WIN_ROWS * GRID_W

    def window(lr):
        r = r0 + lr
        rs = jnp.clip(r - WIN_ROWS // 2, 0, rows - WIN_ROWS)
        off = pl.multiple_of((rs - r0 + HALO_ROWS) * GRID_W, GRID_W)
        return r - rs, off, pl.multiple_of(lr * GRID_W, GRID_W)

    def scores(win, dst, p):
        var, off, qoff = win
        qp = q_ref[pl.ds(qoff, GRID_W), p * 2 * LANES:(p + 1) * 2 * LANES]
        q2 = jnp.concatenate([qp[:, :LANES], qp[:, LANES:]], axis=0)
        kw = kbuf[pl.ds(off, n_keys), p * LANES:(p + 1) * LANES]
        s = lax.dot_general(q2, kw, (((1,), (1,)), ((), ())), preferred_element_type=F32)
        bias = jnp.concatenate([bias_ref[p, WIN_ROWS - 1 - var + 2 * j]
                                for j in range(WIN_ROWS // 2)], axis=-1)
        dst[p] = s + bias

    def attend(win, src, p):
        _, off, qoff = win
        s = src[p]
        m = jnp.max(s, axis=-1, keepdims=True)
        e = jnp.exp2(s - m)
        l = jnp.sum(e, axis=-1, keepdims=True)
        vw = vbuf[pl.ds(off, n_keys), p * LANES:(p + 1) * LANES]
        o2 = jnp.dot(e.astype(BF16), vw, preferred_element_type=F32) / l
        o = jnp.where(lane < HEAD_DIM, o2[:GRID_W], o2[GRID_W:])
        mixed[pl.ds(qoff, GRID_W), p * LANES:(p + 1) * LANES] = o.astype(BF16)

    def row_step(lr, src, dst):
        cur, nxt = window(lr), window(lr + 1)
        scores(nxt, dst, 0)
        scores(nxt, dst, 1)
        for p in range(N_PAIRS):
            attend(cur, src, p)
            if p + 2 < N_PAIRS:
                scores(nxt, dst, p + 2)

    for p in range(N_PAIRS):
        scores(window(0), sbuf_a, p)

    bufs = (sbuf_a, sbuf_b)

    def rows_block(j, carry):
        for a in range(ROW_UNROLL):
            row_step(ROW_UNROLL * j + a, bufs[a % 2], bufs[(a + 1) % 2])
        return carry

    n_blocks = ROWS_PER_TILE // ROW_UNROLL - 1
    lax.fori_loop(0, n_blocks, rows_block, 0)
    for lr in range(n_blocks * ROW_UNROLL, ROWS_PER_TILE - 1):
        row_step(lr, bufs[lr % 2], bufs[(lr + 1) % 2])
    for p in range(N_PAIRS):
        attend(window(ROWS_PER_TILE - 1), bufs[(ROWS_PER_TILE - 1) % 2], p)

    ext = TM + 4 * SUB
    inner = TM + 2 * SUB
    centre = 2 * SUB
    ubuf[0:SUB, :] = jnp.zeros((SUB, POOL_WIDTH), F32)
    ubuf[ext - SUB:ext, :] = jnp.zeros((SUB, POOL_WIDTH), F32)
    ubuf[SUB:centre, :] = jnp.where(ti > 0, up_ref[...].astype(F32)[SUB:], 0.0)
    ubuf[centre:centre + TM, :] = uc_ref[...].astype(F32)
    ubuf[centre + TM:ext - SUB, :] = jnp.where(ti < tps - 1, un_ref[...].astype(F32)[:SUB], 0.0)
    lvl[:, :, 0:SUB, :] = jnp.zeros(lvl.shape[:2] + (SUB, POOL_GROUP), F32)
    lvl[:, :, ext - SUB:ext, :] = jnp.zeros(lvl.shape[:2] + (SUB, POOL_GROUP), F32)
    tpos = ti * TM + lax.broadcasted_iota(jnp.int32, (TM, POOL_GROUP), 0)
    for g, w in enumerate(POOL_WINDOWS):
        cols = slice(g * POOL_GROUP, (g + 1) * POOL_GROUP)
        if w == 2:
            acc = ubuf[pl.ds(centre - 1, TM), cols] + ubuf[pl.ds(centre, TM), cols]
        else:
            lv = lvl.at[g - 1]
            lv[0, SUB:SUB + inner, :] = (ubuf[pl.ds(SUB - 1, inner), cols]
                                         + ubuf[pl.ds(SUB, inner), cols])
            half, k = 1, 0
            while 4 * half < w:
                lv[1 - k, SUB:SUB + inner, :] = (lv[k, pl.ds(SUB - half, inner), :]
                                                 + lv[k, pl.ds(SUB + half, inner), :])
                half, k = 2 * half, 1 - k
            acc = lv[k, pl.ds(centre - half, TM), :] + lv[k, pl.ds(centre + half, TM), :]
        lo = jnp.maximum(tpos - w // 2, 0)
        hi = jnp.minimum(tpos - w // 2 + w, seq)
        cnt = (hi - lo).astype(F32)
        pooled = (acc / cnt - ubuf[centre:centre + TM, cols]).astype(BF16)
        mixed[:, ATT_WIDTH + g * POOL_GROUP:ATT_WIDTH + (g + 1) * POOL_GROUP] = pooled

    out = jnp.dot(mixed[...], wout_ref[...], preferred_element_type=F32)
    o_ref[...] = x_ref[...] + mod_ref[0][2:3] * out


def _fold_kernel(wp_ref, ps_ref, wo_ref, o_ref):
    w = wp_ref[0] * ps_ref[0]
    o_ref[...] = jnp.dot(w, wo_ref[...], preferred_element_type=F32,
                         precision=lax.Precision.HIGHEST)


def _fold_pool(w_pool, pool_scale, w_out):
    n_g = len(POOL_WINDOWS)
    first = ATT_WIDTH // POOL_GROUP
    return pl.pallas_call(
        _fold_kernel,
        grid=(n_g,),
        in_specs=[pl.BlockSpec((1, POOL_GROUP, POOL_GROUP), lambda g: (g, 0, 0)),
                  pl.BlockSpec((1, 1, POOL_GROUP), lambda g: (g, 0, 0)),
                  pl.BlockSpec((POOL_GROUP, D_MODEL), lambda g: (first + g, 0))],
        out_specs=pl.BlockSpec((POOL_GROUP, D_MODEL), lambda g: (g, 0)),
        out_shape=jax.ShapeDtypeStruct((POOL_WIDTH, D_MODEL), F32),
        compiler_params=_params(),
        name="fold_pool",
    )(w_pool, pool_scale.reshape(n_g, 1, POOL_GROUP), w_out)


def _mix(x2, qx, k, v, u, mod3, boff, seq, bias, w_out):
    n = x2.shape[0]
    tps = seq // TM
    rows = seq // GRID_W
    hb = TM // HALO_TOK
    n_hb = n // HALO_TOK
    ub = TM // SUB16
    n_ub = n // SUB16
    cur = lambda i: (i, 0)
    kprev = lambda i: (jnp.maximum(i * hb - 1, 0), 0)
    knext = lambda i: (jnp.minimum((i + 1) * hb, n_hb - 1), 0)
    uprev = lambda i: (jnp.maximum(i * ub - 1, 0), 0)
    unext = lambda i: (jnp.minimum((i + 1) * ub, n_ub - 1), 0)
    kernel = functools.partial(_mix_kernel, tps=tps, rows=rows, seq=seq)
    return pl.pallas_call(
        kernel,
        grid=(n // TM,),
        in_specs=[pl.BlockSpec((TM, 2 * ATT_WIDTH), cur),
                  pl.BlockSpec((HALO_TOK, ATT_WIDTH), kprev),
                  pl.BlockSpec((TM, ATT_WIDTH), cur),
                  pl.BlockSpec((HALO_TOK, ATT_WIDTH), knext),
                  pl.BlockSpec((HALO_TOK, ATT_WIDTH), kprev),
                  pl.BlockSpec((TM, ATT_WIDTH), cur),
                  pl.BlockSpec((HALO_TOK, ATT_WIDTH), knext),
                  pl.BlockSpec((SUB16, POOL_WIDTH), uprev),
                  pl.BlockSpec((TM, POOL_WIDTH), cur),
                  pl.BlockSpec((SUB16, POOL_WIDTH), unext),
                  pl.BlockSpec((TM, D_MODEL), cur),
                  pl.BlockSpec((1, 6, D_MODEL), lambda i: (boff + i // tps, 0, 0)),
                  _const_spec(bias.shape),
                  _const_spec(w_out.shape)],
        out_specs=pl.BlockSpec((TM, D_MODEL), cur),
        out_shape=jax.ShapeDtypeStruct((n, D_MODEL), F32),
        scratch_shapes=[pltpu.VMEM((TM + 2 * HALO_TOK, ATT_WIDTH), BF16),
                        pltpu.VMEM((TM + 2 * HALO_TOK, ATT_WIDTH), BF16),
                        pltpu.VMEM((TM, D_MODEL), BF16),
                        pltpu.VMEM((N_PAIRS, 2 * GRID_W, WIN_ROWS * GRID_W), F32),
                        pltpu.VMEM((N_PAIRS, 2 * GRID_W, WIN_ROWS * GRID_W), F32),
                        pltpu.VMEM((TM + 4 * SUB, POOL_WIDTH), F32),
                        pltpu.VMEM((len(POOL_WINDOWS) - 1, 2, TM + 4 * SUB, POOL_GROUP), F32)],
        compiler_params=_params(),
        name="mix",
    )(qx, k, k, k, v, v, v, u, u, u, x2, mod3, bias, w_out)


def _mlp_kernel(xp_ref, xc_ref, xn_ref, mod_ref, g_ref, wup_ref, cw_ref, cb_ref, wdown_ref,
                o_ref, act, *, tps, tm):
    ti = lax.rem(pl.program_id(0), tps)
    mod = mod_ref[0]
    shift, scale, gate = mod[3:4], mod[4:5], mod[5:6]
    g = g_ref[...]
    xc = xc_ref[...]
    hp = jnp.where(ti > 0, _norm_mod(xp_ref[...], g, scale, shift), 0.0)
    hn = jnp.where(ti < tps - 1, _norm_mod(xn_ref[...], g, scale, shift), 0.0)
    h = jnp.concatenate([hp, _norm_mod(xc, g, scale, shift), hn], axis=0).astype(BF16)
    ext = tm + 2 * SUB

    def conv(cols):
        up = jnp.dot(h, wup_ref[:, cols], preferred_element_type=F32)
        cw = cw_ref[:, cols]
        prev = pltpu.roll(up, 1, 0)[SUB:SUB + tm]
        nxt = pltpu.roll(up, ext - 1, 0)[SUB:SUB + tm]
        return prev * cw[0:1] + up[SUB:SUB + tm] * cw[1:2] + nxt * cw[2:3] + cb_ref[:, cols]

    for lo in range(0, D_FF, FC):
        hi = min(lo + FC, D_FF)
        gt = conv(slice(lo, hi))
        vl = conv(slice(D_FF + lo, D_FF + hi))
        act[:, lo:hi] = ((gt * jax.nn.sigmoid(gt)) * vl).astype(BF16)

    y = jnp.dot(act[...], wdown_ref[...], preferred_element_type=F32)
    o_ref[...] = xc + gate * y


def _mlp(x1, mod3, boff, seq, g2, w_up, conv_w, conv_b, w_down):
    n = x1.shape[0]
    tm = TM_MLP
    tps = seq // tm
    hb = tm // SUB
    n_hb = n // SUB
    kernel = functools.partial(_mlp_kernel, tps=tps, tm=tm)
    return pl.pallas_call(
        kernel,
        grid=(n // tm,),
        in_specs=[pl.BlockSpec((SUB, D_MODEL), lambda i: (jnp.maximum(i * hb - 1, 0), 0)),
                  pl.BlockSpec((tm, D_MODEL), lambda i: (i, 0)),
                  pl.BlockSpec((SUB, D_MODEL), lambda i: (jnp.minimum((i + 1) * hb, n_hb - 1), 0)),
                  pl.BlockSpec((1, 6, D_MODEL), lambda i: (boff + i // tps, 0, 0)),
                  _const_spec((1, D_MODEL)),
                  _const_spec(w_up.shape),
                  _const_spec(conv_w.shape),
                  _const_spec((1, 2 * D_FF)),
                  _const_spec(w_down.shape)],
        out_specs=pl.BlockSpec((tm, D_MODEL), lambda i: (i, 0)),
        out_shape=jax.ShapeDtypeStruct((n, D_MODEL), F32),
        scratch_shapes=[pltpu.VMEM((tm, D_FF), BF16)],
        compiler_params=_params(),
        name="mlp",
    )(x1, x1, x1, mod3, g2, w_up, conv_w, conv_b, w_down)


def _bias_table(rpb):
    cols = np.arange(GRID_W)
    start = np.clip(cols - WIN_COLS // 2, 0, GRID_W - WIN_COLS)
    onehot = np.zeros((2 * WIN_COLS - 1, GRID_W, GRID_W), np.float32)
    inside = np.zeros((GRID_W, GRID_W), bool)
    for c in cols:
        for j in range(WIN_COLS):
            c2 = start[c] + j
            onehot[c2 - c + WIN_COLS - 1, c, c2] = 1.0
            inside[c, c2] = True
    t = jnp.einsum('hrd,dcx->hrcx', rpb.astype(F32), jnp.asarray(onehot),
                   precision=lax.Precision.HIGHEST)
    t = jnp.where(jnp.asarray(inside), t * LOG2E, NEG)
    two = jnp.concatenate([t[:, :-1], t[:, 1:]], axis=-1)
    two = two.reshape(N_PAIRS, 2, 2 * WIN_ROWS - 2, GRID_W, 2 * GRID_W).transpose(0, 2, 1, 3, 4)
    return two.reshape(N_PAIRS, 2 * WIN_ROWS - 2, 2 * GRID_W, 2 * GRID_W)


def _layer(xs, mod3, p):
    (norm1_g, norm2_g, w_in, q_norm_g, k_norm_g, rpb, w_pool, pool_scale, w_out,
     w_up, conv_w, conv_b, w_down) = p
    g1 = norm1_g.reshape(1, D_MODEL)
    g2 = norm2_g.reshape(1, D_MODEL)
    w_in_b = w_in.astype(BF16)
    heads = np.arange(MXU_N) // HEAD_DIM
    seg = jnp.asarray((heads[:, None] == heads[None, :]).astype(np.float32) / HEAD_DIM, BF16)
    qg = (jnp.tile(q_norm_g, N_HEADS) * (HEAD_DIM ** -0.5 * LOG2E)).reshape(1, ATT_WIDTH)
    kg = jnp.tile(k_norm_g, N_HEADS).reshape(1, ATT_WIDTH)
    bias = _bias_table(rpb)
    w_out_b = jnp.concatenate([w_out[:ATT_WIDTH], _fold_pool(w_pool, pool_scale, w_out)],
                              axis=0).astype(BF16)
    w_up_b = w_up.astype(BF16)
    cb = conv_b.reshape(1, 2 * D_FF)
    w_down_b = w_down.astype(BF16)

    outs = []
    boff = 0
    for x in xs:
        b, seq, _ = x.shape
        x2 = x.reshape(b * seq, D_MODEL)
        qx, k, v, u = _inproj(x2, mod3, boff, seq, g1, w_in_b, seg, qg, kg)
        x1 = _mix(x2, qx, k, v, u, mod3, boff, seq, bias, w_out_b)
        y = _mlp(x1, mod3, boff, seq, g2, w_up_b, conv_w, cb, w_down_b)
        outs.append(y.reshape(b, seq, D_MODEL))
        boff += b
    return outs


def kernel(x_prompt, x_sample, c_prompt, c_sample, w_ada, b_ada, norm1_g, norm2_g, w_in,
           q_norm_g, k_norm_g, rpb, w_pool, pool_scale, w_out, w_up, conv_w, conv_b, w_down):
    xs = [x_prompt, x_sample]
    c = jnp.concatenate([c_prompt, c_sample], axis=0)
    n_b = c.shape[0]
    c = jnp.pad(c, ((0, -n_b % SUB16), (0, 0)))
    for l in range(w_ada.shape[0]):
        mod = _modulation(c, w_ada[l], b_ada[l])
        mod3 = mod.reshape(c.shape[0], 6, D_MODEL)
        p = (norm1_g[l], norm2_g[l], w_in[l], q_norm_g[l], k_norm_g[l], rpb[l], w_pool[l],
             pool_scale[l], w_out[l], w_up[l], conv_w[l], conv_b[l], w_down[l])
        xs = _layer(xs, mod3, p)
    return (xs[0], xs[1])
```

```python
import functools

import numpy as np
import jax
import jax.numpy as jnp
from jax import lax
from jax.experimental import pallas as pl
from jax.experimental.pallas import tpu as pltpu

F32 = jnp.float32
BF16 = jnp.bfloat16

D_MODEL = 1024
GRID_W = 64
HEAD_DIM = 64
N_HEADS = 8
N_PAIRS = N_HEADS // 2
ATT_WIDTH = N_HEADS * HEAD_DIM
POOL_WINDOWS = (2, 4, 8, 16)
POOL_GROUP = 128
POOL_WIDTH = POOL_GROUP * len(POOL_WINDOWS)
WIN_ROWS = 8
WIN_COLS = 16
D_FF = 2816
EPS = 1e-6
NEG = -1e30
LOG2E = 1.4426950408889634

LANES = 128
MXU_N = 256
TM = 1024
ROWS_PER_TILE = TM // GRID_W
ROW_UNROLL = 4
TM_MLP = 1024
TM_IN = 1024
HALO_ROWS = 4
HALO_TOK = HALO_ROWS * GRID_W
SUB = 8
SUB16 = 16
FC = 512
VMEM_LIMIT = 56 * 1024 * 1024


def _const_spec(shape):
    nd = len(shape)
    return pl.BlockSpec(shape, lambda i: (0,) * nd, pipeline_mode=pl.Buffered(1))


def _params(vmem=VMEM_LIMIT):
    return pltpu.CompilerParams(dimension_semantics=("arbitrary",), vmem_limit_bytes=vmem)


def _ada_kernel(c_ref, w_ref, b_ref, o_ref):
    c = c_ref[...]
    s = c * jax.nn.sigmoid(c)
    hi = s.astype(BF16)
    lo = (s - hi.astype(F32)).astype(BF16)
    r = jnp.dot(jnp.concatenate([hi, lo], axis=0), w_ref[...].astype(BF16),
                preferred_element_type=F32)
    n = c.shape[0]
    o_ref[...] = (r[:n] + r[n:]) + b_ref[...]


def _modulation(c, w_ada, b_ada):
    rows = c.shape[0]
    n = w_ada.shape[1]
    bn = 1536
    return pl.pallas_call(
        _ada_kernel,
        grid=(n // bn,),
        in_specs=[pl.BlockSpec((rows, D_MODEL), lambda j: (0, 0)),
                  pl.BlockSpec((D_MODEL, bn), lambda j: (0, j)),
                  pl.BlockSpec((1, bn), lambda j: (0, j))],
        out_specs=pl.BlockSpec((rows, bn), lambda j: (0, j)),
        out_shape=jax.ShapeDtypeStruct((rows, n), F32),
        compiler_params=_params(),
        name="modulation",
    )(c, w_ada, b_ada.reshape(1, n))


def _norm_mod(x, g, scale, shift):
    ms = jnp.mean(x * x, axis=-1, keepdims=True)
    return ((x * lax.rsqrt(ms + EPS)) * g) * (1.0 + scale) + shift


def _inproj_kernel(x_ref, mod_ref, g_ref, w_ref, seg_ref, qg_ref, kg_ref,
                   qx_ref, k_ref, v_ref, u_ref):
    mod = mod_ref[0]
    h = _norm_mod(x_ref[...], g_ref[...], mod[1:2], mod[0:1]).astype(BF16)
    lane = lax.broadcasted_iota(jnp.int32, (TM_IN, LANES), 1)
    first_head = lane < HEAD_DIM

    def proj(c):
        return jnp.dot(h, w_ref[:, c * MXU_N:(c + 1) * MXU_N], preferred_element_type=F32)

    def head_norm(y, gain):
        ms = jnp.dot((y * y).astype(BF16), seg_ref[...], preferred_element_type=F32)
        return (y * lax.rsqrt(ms + EPS)) * gain

    n_chunks = ATT_WIDTH // MXU_N
    qk = [proj(c) for c in range(2 * n_chunks)]
    for j in range(2 * n_chunks):
        other = proj(2 * n_chunks + j).astype(BF16)
        if j < n_chunks:
            v_ref[:, j * MXU_N:(j + 1) * MXU_N] = other
            qn = head_norm(qk[j], qg_ref[:, j * MXU_N:(j + 1) * MXU_N])
            for pp in range(MXU_N // LANES):
                pair = qn[:, pp * LANES:(pp + 1) * LANES]
                base = (j * (MXU_N // LANES) + pp) * 2 * LANES
                qx_ref[:, base:base + LANES] = jnp.where(first_head, pair, 0.0).astype(BF16)
                qx_ref[:, base + LANES:base + 2 * LANES] = jnp.where(first_head, 0.0, pair).astype(BF16)
        else:
            c = j - n_chunks
            u_ref[:, c * MXU_N:(c + 1) * MXU_N] = other
            kn = head_norm(qk[j], kg_ref[:, c * MXU_N:(c + 1) * MXU_N])
            k_ref[:, c * MXU_N:(c + 1) * MXU_N] = kn.astype(BF16)


def _inproj(x2, mod3, boff, seq, g1, w_in, seg, qg, kg):
    n = x2.shape[0]
    tps = seq // TM_IN
    out_w = (2 * ATT_WIDTH, ATT_WIDTH, ATT_WIDTH, POOL_WIDTH)
    return pl.pallas_call(
        _inproj_kernel,
        grid=(n // TM_IN,),
        in_specs=[pl.BlockSpec((TM_IN, D_MODEL), lambda i: (i, 0)),
                  pl.BlockSpec((1, 6, D_MODEL), lambda i: (boff + i // tps, 0, 0)),
                  _const_spec((1, D_MODEL)),
                  _const_spec(w_in.shape),
                  _const_spec(seg.shape),
                  _const_spec((1, ATT_WIDTH)),
                  _const_spec((1, ATT_WIDTH))],
        out_specs=[pl.BlockSpec((TM_IN, w), lambda i: (i, 0)) for w in out_w],
        out_shape=[jax.ShapeDtypeStruct((n, w), BF16) for w in out_w],
        compiler_params=_params(),
        name="inproj",
    )(x2, mod3, g1, w_in, seg, qg, kg)


def _mix_kernel(q_ref, kp_ref, kc_ref, kn_ref, vp_ref, vc_ref, vn_ref,
                up_ref, uc_ref, un_ref, x_ref, mod_ref, bias_ref,
                wout_ref, o_ref,
                kbuf, vbuf, mixed, sbuf_a, sbuf_b, ubuf, lvl, *, tps, rows, seq):
    ti = lax.rem(pl.program_id(0), tps)
    r0 = ti * ROWS_PER_TILE

    kbuf[0:HALO_TOK, :] = kp_ref[...]
    kbuf[HALO_TOK:HALO_TOK + TM, :] = kc_ref[...]
    kbuf[HALO_TOK + TM:, :] = kn_ref[...]
    vbuf[0:HALO_TOK, :] = vp_ref[...]
    vbuf[HALO_TOK:HALO_TOK + TM, :] = vc_ref[...]
    vbuf[HALO_TOK + TM:, :] = vn_ref[...]

    lane = lax.broadcasted_iota(jnp.int32, (GRID_W, LANES), 1)
    n_keys = WIN_ROWS * GRID_W

    def window(lr):
        r = r0 + lr
        rs = jnp.clip(r - WIN_ROWS // 2, 0, rows - WIN_ROWS)
        off = pl.multiple_of((rs - r0 + HALO_ROWS) * GRID_W, GRID_W)
        return r - rs, off, pl.multiple_of(lr * GRID_W, GRID_W)

    def scores(win, dst, p):
        var, off, qoff = win
        qp = q_ref[pl.ds(qoff, GRID_W), p * 2 * LANES:(p + 1) * 2 * LANES]
        q2 = jnp.concatenate([qp[:, :LANES], qp[:, LANES:]], axis=0)
        kw = kbuf[pl.ds(off, n_keys), p * LANES:(p + 1) * LANES]
        s = lax.dot_general(q2, kw, (((1,), (1,)), ((), ())), preferred_element_type=F32)
        bias = jnp.concatenate([bias_ref[p, WIN_ROWS - 1 - var + 2 * j]
                                for j in range(WIN_ROWS // 2)], axis=-1)
        dst[p] = s + bias

    def attend(win, src, p):
        _, off, qoff = win
        s = src[p]
        m = jnp.max(s, axis=-1, keepdims=True)
        e = jnp.exp2(s - m)
        l = jnp.sum(e, axis=-1, keepdims=True)
        vw = vbuf[pl.ds(off, n_keys), p * LANES:(p + 1) * LANES]
        o2 = jnp.dot(e.astype(BF16), vw, preferred_element_type=F32) / l
        o = jnp.where(lane < HEAD_DIM, o2[:GRID_W], o2[GRID_W:])
        mixed[pl.ds(qoff, GRID_W), p * LANES:(p + 1) * LANES] = o.astype(BF16)

    def row_step(lr, src, dst):
        cur, nxt = window(lr), window(lr + 1)
        scores(nxt, dst, 0)
        scores(nxt, dst, 1)
        for p in range(N_PAIRS):
            attend(cur, src, p)
            if p + 2 < N_PAIRS:
                scores(nxt, dst, p + 2)

    for p in range(N_PAIRS):
        scores(window(0), sbuf_a, p)

    bufs = (sbuf_a, sbuf_b)

    def rows_block(j, carry):
        for a in range(ROW_UNROLL):
            row_step(ROW_UNROLL * j + a, bufs[a % 2], bufs[(a + 1) % 2])
        return carry

    n_blocks = ROWS_PER_TILE // ROW_UNROLL - 1
    lax.fori_loop(0, n_blocks, rows_block, 0)
    for lr in range(n_blocks * ROW_UNROLL, ROWS_PER_TILE - 1):
        row_step(lr, bufs[lr % 2], bufs[(lr + 1) % 2])
    for p in range(N_PAIRS):
        attend(window(ROWS_PER_TILE - 1), bufs[(ROWS_PER_TILE - 1) % 2], p)

    ext = TM + 4 * SUB
    inner = TM + 2 * SUB
    centre = 2 * SUB
    ubuf[0:SUB, :] = jnp.zeros((SUB, POOL_WIDTH), F32)
    ubuf[ext - SUB:ext, :] = jnp.zeros((SUB, POOL_WIDTH), F32)
    ubuf[SUB:centre, :] = jnp.where(ti > 0, up_ref[...].astype(F32)[SUB:], 0.0)
    ubuf[centre:centre + TM, :] = uc_ref[...].astype(F32)
    ubuf[centre + TM:ext - SUB, :] = jnp.where(ti < tps - 1, un_ref[...].astype(F32)[:SUB], 0.0)
    lvl[:, :, 0:SUB, :] = jnp.zeros(lvl.shape[:2] + (SUB, POOL_GROUP), F32)
    lvl[:, :, ext - SUB:ext, :] = jnp.zeros(lvl.shape[:2] + (SUB, POOL_GROUP), F32)
    tpos = ti * TM + lax.broadcasted_iota(jnp.int32, (TM, POOL_GROUP), 0)
    for g, w in enumerate(POOL_WINDOWS):
        cols = slice(g * POOL_GROUP, (g + 1) * POOL_GROUP)
        if w == 2:
            acc = ubuf[pl.ds(centre - 1, TM), cols] + ubuf[pl.ds(centre, TM), cols]
        else:
            lv = lvl.at[g - 1]
            lv[0, SUB:SUB + inner, :] = (ubuf[pl.ds(SUB - 1, inner), cols]
                                         + ubuf[pl.ds(SUB, inner), cols])
            half, k = 1, 0
            while 4 * half < w:
                lv[1 - k, SUB:SUB + inner, :] = (lv[k, pl.ds(SUB - half, inner), :]
                                                 + lv[k, pl.ds(SUB + half, inner), :])
                half, k = 2 * half, 1 - k
            acc = lv[k, pl.ds(centre - half, TM), :] + lv[k, pl.ds(centre + half, TM), :]
        lo = jnp.maximum(tpos - w // 2, 0)
        hi = jnp.minimum(tpos - w // 2 + w, seq)
        cnt = (hi - lo).astype(F32)
        pooled = (acc / cnt - ubuf[centre:centre + TM, cols]).astype(BF16)
        mixed[:, ATT_WIDTH + g * POOL_GROUP:ATT_WIDTH + (g + 1) * POOL_GROUP] = pooled

    out = jnp.dot(mixed[...], wout_ref[...], preferred_element_type=F32)
    o_ref[...] = x_ref[...] + mod_ref[0][2:3] * out


def _fold_kernel(wp_ref, ps_ref, wo_ref, o_ref):
    w = wp_ref[0] * ps_ref[0]
    o_ref[...] = jnp.dot(w, wo_ref[...], preferred_element_type=F32,
                         precision=lax.Precision.HIGHEST)


def _fold_pool(w_pool, pool_scale, w_out):
    n_g = len(POOL_WINDOWS)
    first = ATT_WIDTH // POOL_GROUP
    return pl.pallas_call(
        _fold_kernel,
        grid=(n_g,),
        in_specs=[pl.BlockSpec((1, POOL_GROUP, POOL_GROUP), lambda g: (g, 0, 0)),
                  pl.BlockSpec((1, 1, POOL_GROUP), lambda g: (g, 0, 0)),
                  pl.BlockSpec((POOL_GROUP, D_MODEL), lambda g: (first + g, 0))],
        out_specs=pl.BlockSpec((POOL_GROUP, D_MODEL), lambda g: (g, 0)),
        out_shape=jax.ShapeDtypeStruct((POOL_WIDTH, D_MODEL), F32),
        compiler_params=_params(),
        name="fold_pool",
    )(w_pool, pool_scale.reshape(n_g, 1, POOL_GROUP), w_out)


def _mix(x2, qx, k, v, u, mod3, boff, seq, bias, w_out):
    n = x2.shape[0]
    tps = seq // TM
    rows = seq // GRID_W
    hb = TM // HALO_TOK
    n_hb = n // HALO_TOK
    ub = TM // SUB16
    n_ub = n // SUB16
    cur = lambda i: (i, 0)
    kprev = lambda i: (jnp.maximum(i * hb - 1, 0), 0)
    knext = lambda i: (jnp.minimum((i + 1) * hb, n_hb - 1), 0)
    uprev = lambda i: (jnp.maximum(i * ub - 1, 0), 0)
    unext = lambda i: (jnp.minimum((i + 1) * ub, n_ub - 1), 0)
    kernel = functools.partial(_mix_kernel, tps=tps, rows=rows, seq=seq)
    return pl.pallas_call(
        kernel,
        grid=(n // TM,),
        in_specs=[pl.BlockSpec((TM, 2 * ATT_WIDTH), cur),
                  pl.BlockSpec((HALO_TOK, ATT_WIDTH), kprev),
                  pl.BlockSpec((TM, ATT_WIDTH), cur),
                  pl.BlockSpec((HALO_TOK, ATT_WIDTH), knext),
                  pl.BlockSpec((HALO_TOK, ATT_WIDTH), kprev),
                  pl.BlockSpec((TM, ATT_WIDTH), cur),
                  pl.BlockSpec((HALO_TOK, ATT_WIDTH), knext),
                  pl.BlockSpec((SUB16, POOL_WIDTH), uprev),
                  pl.BlockSpec((TM, POOL_WIDTH), cur),
                  pl.BlockSpec((SUB16, POOL_WIDTH), unext),
                  pl.BlockSpec((TM, D_MODEL), cur),
                  pl.BlockSpec((1, 6, D_MODEL), lambda i: (boff + i // tps, 0, 0)),
                  _const_spec(bias.shape),
                  _const_spec(w_out.shape)],
        out_specs=pl.BlockSpec((TM, D_MODEL), cur),
        out_shape=jax.ShapeDtypeStruct((n, D_MODEL), F32),
        scratch_shapes=[pltpu.VMEM((TM + 2 * HALO_TOK, ATT_WIDTH), BF16),
                        pltpu.VMEM((TM + 2 * HALO_TOK, ATT_WIDTH), BF16),
                        pltpu.VMEM((TM, D_MODEL), BF16),
                        pltpu.VMEM((N_PAIRS, 2 * GRID_W, WIN_ROWS * GRID_W), F32),
                        pltpu.VMEM((N_PAIRS, 2 * GRID_W, WIN_ROWS * GRID_W), F32),
                        pltpu.VMEM((TM + 4 * SUB, POOL_WIDTH), F32),
                        pltpu.VMEM((len(POOL_WINDOWS) - 1, 2, TM + 4 * SUB, POOL_GROUP), F32)],
        compiler_params=_params(),
        name="mix",
    )(qx, k, k, k, v, v, v, u, u, u, x2, mod3, bias, w_out)


def _mlp_kernel(xp_ref, xc_ref, xn_ref, mod_ref, g_ref, wup_ref, cw_ref, cb_ref, wdown_ref,
                o_ref, act, *, tps, tm):
    ti = lax.rem(pl.program_id(0), tps)
    mod = mod_ref[0]
    shift, scale, gate = mod[3:4], mod[4:5], mod[5:6]
    g = g_ref[...]
    xc = xc_ref[...]
    hp = jnp.where(ti > 0, _norm_mod(xp_ref[...], g, scale, shift), 0.0)
    hn = jnp.where(ti < tps - 1, _norm_mod(xn_ref[...], g, scale, shift), 0.0)
    h = jnp.concatenate([hp, _norm_mod(xc, g, scale, shift), hn], axis=0).astype(BF16)
    ext = tm + 2 * SUB

    def conv(cols):
        up = jnp.dot(h, wup_ref[:, cols], preferred_element_type=F32)
        cw = cw_ref[:, cols]
        prev = pltpu.roll(up, 1, 0)[SUB:SUB + tm]
        nxt = pltpu.roll(up, ext - 1, 0)[SUB:SUB + tm]
        return prev * cw[0:1] + up[SUB:SUB + tm] * cw[1:2] + nxt * cw[2:3] + cb_ref[:, cols]

    for lo in range(0, D_FF, FC):
        hi = min(lo + FC, D_FF)
        gt = conv(slice(lo, hi))
        vl = conv(slice(D_FF + lo, D_FF + hi))
        gb, vb = gt.astype(BF16), vl.astype(BF16)
        act[:, lo:hi] = (gb * jax.nn.sigmoid(gb)) * vb

    y = jnp.dot(act[...], wdown_ref[...], preferred_element_type=F32)
    o_ref[...] = xc + gate * y


def _mlp(x1, mod3, boff, seq, g2, w_up, conv_w, conv_b, w_down):
    n = x1.shape[0]
    tm = TM_MLP
    tps = seq // tm
    hb = tm // SUB
    n_hb = n // SUB
    kernel = functools.partial(_mlp_kernel, tps=tps, tm=tm)
    return pl.pallas_call(
        kernel,
        grid=(n // tm,),
        in_specs=[pl.BlockSpec((SUB, D_MODEL), lambda i: (jnp.maximum(i * hb - 1, 0), 0)),
                  pl.BlockSpec((tm, D_MODEL), lambda i: (i, 0)),
                  pl.BlockSpec((SUB, D_MODEL), lambda i: (jnp.minimum((i + 1) * hb, n_hb - 1), 0)),
                  pl.BlockSpec((1, 6, D_MODEL), lambda i: (boff + i // tps, 0, 0)),
                  _const_spec((1, D_MODEL)),
                  _const_spec(w_up.shape),
                  _const_spec(conv_w.shape),
                  _const_spec((1, 2 * D_FF)),
                  _const_spec(w_down.shape)],
        out_specs=pl.BlockSpec((tm, D_MODEL), lambda i: (i, 0)),
        out_shape=jax.ShapeDtypeStruct((n, D_MODEL), F32),
        scratch_shapes=[pltpu.VMEM((tm, D_FF), BF16)],
        compiler_params=_params(),
        name="mlp",
    )(x1, x1, x1, mod3, g2, w_up, conv_w, conv_b, w_down)


def _bias_table(rpb):
    cols = np.arange(GRID_W)
    start = np.clip(cols - WIN_COLS // 2, 0, GRID_W - WIN_COLS)
    onehot = np.zeros((2 * WIN_COLS - 1, GRID_W, GRID_W), np.float32)
    inside = np.zeros((GRID_W, GRID_W), bool)
    for c in cols:
        for j in range(WIN_COLS):
            c2 = start[c] + j
            onehot[c2 - c + WIN_COLS - 1, c, c2] = 1.0
            inside[c, c2] = True
    t = jnp.einsum('hrd,dcx->hrcx', rpb.astype(F32), jnp.asarray(onehot),
                   precision=lax.Precision.HIGHEST)
    t = jnp.where(jnp.asarray(inside), t * LOG2E, NEG)
    two = jnp.concatenate([t[:, :-1], t[:, 1:]], axis=-1)
    two = two.reshape(N_PAIRS, 2, 2 * WIN_ROWS - 2, GRID_W, 2 * GRID_W).transpose(0, 2, 1, 3, 4)
    return two.reshape(N_PAIRS, 2 * WIN_ROWS - 2, 2 * GRID_W, 2 * GRID_W)


def _layer(xs, mod3, p):
    (norm1_g, norm2_g, w_in, q_norm_g, k_norm_g, rpb, w_pool, pool_scale, w_out,
     w_up, conv_w, conv_b, w_down) = p
    g1 = norm1_g.reshape(1, D_MODEL)
    g2 = norm2_g.reshape(1, D_MODEL)
    w_in_b = w_in.astype(BF16)
    heads = np.arange(MXU_N) // HEAD_DIM
    seg = jnp.asarray((heads[:, None] == heads[None, :]).astype(np.float32) / HEAD_DIM, BF16)
    qg = (jnp.tile(q_norm_g, N_HEADS) * (HEAD_DIM ** -0.5 * LOG2E)).reshape(1, ATT_WIDTH)
    kg = jnp.tile(k_norm_g, N_HEADS).reshape(1, ATT_WIDTH)
    bias = _bias_table(rpb)
    w_out_b = jnp.concatenate([w_out[:ATT_WIDTH], _fold_pool(w_pool, pool_scale, w_out)],
                              axis=0).astype(BF16)
    w_up_b = w_up.astype(BF16)
    cb = conv_b.reshape(1, 2 * D_FF)
    w_down_b = w_down.astype(BF16)

    outs = []
    boff = 0
    for x in xs:
        b, seq, _ = x.shape
        x2 = x.reshape(b * seq, D_MODEL)
        qx, k, v, u = _inproj(x2, mod3, boff, seq, g1, w_in_b, seg, qg, kg)
        x1 = _mix(x2, qx, k, v, u, mod3, boff, seq, bias, w_out_b)
        y = _mlp(x1, mod3, boff, seq, g2, w_up_b, conv_w, cb, w_down_b)
        outs.append(y.reshape(b, seq, D_MODEL))
        boff += b
    return outs


def kernel(x_prompt, x_sample, c_prompt, c_sample, w_ada, b_ada, norm1_g, norm2_g, w_in,
           q_norm_g, k_norm_g, rpb, w_pool, pool_scale, w_out, w_up, conv_w, conv_b, w_down):
    xs = [x_prompt, x_sample]
    c = jnp.concatenate([c_prompt, c_sample], axis=0)
    n_b = c.shape[0]
    c = jnp.pad(c, ((0, -n_b % SUB16), (0, 0)))
    for l in range(w_ada.shape[0]):
        mod = _modulation(c, w_ada[l], b_ada[l])
        mod3 = mod.reshape(c.shape[0], 6, D_MODEL)
        p = (norm1_g[l], norm2_g[l], w_in[l], q_norm_g[l], k_norm_g[l], rpb[l], w_pool[l],
             pool_scale[l], w_out[l], w_up[l], conv_w[l], conv_b[l], w_down[l])
        xs = _layer(xs, mod3, p)
    return (xs[0], xs[1])
```

```python
import functools

import numpy as np
import jax
import jax.numpy as jnp
from jax import lax
from jax.experimental import pallas as pl
from jax.experimental.pallas import tpu as pltpu

F32 = jnp.float32
BF16 = jnp.bfloat16

D_MODEL = 1024
GRID_W = 64
HEAD_DIM = 64
N_HEADS = 8
N_PAIRS = N_HEADS // 2
ATT_WIDTH = N_HEADS * HEAD_DIM
POOL_WINDOWS = (2, 4, 8, 16)
POOL_GROUP = 128
POOL_WIDTH = POOL_GROUP * len(POOL_WINDOWS)
WIN_ROWS = 8
WIN_COLS = 16
D_FF = 2816
EPS = 1e-6
NEG = -1e30
LOG2E = 1.4426950408889634

LANES = 128
MXU_N = 256
TM = 1024
ROWS_PER_TILE = TM // GRID_W
ROW_UNROLL = 4
TM_MLP = 1024
TM_IN = 1024
HALO_ROWS = 4
HALO_TOK = HALO_ROWS * GRID_W
SUB = 8
SUB16 = 16
FC = 512
VMEM_LIMIT = 56 * 1024 * 1024


def _const_spec(shape):
    nd = len(shape)
    return pl.BlockSpec(shape, lambda i: (0,) * nd, pipeline_mode=pl.Buffered(1))


def _params(vmem=VMEM_LIMIT):
    return pltpu.CompilerParams(dimension_semantics=("arbitrary",), vmem_limit_bytes=vmem)


def _ada_kernel(c_ref, w_ref, b_ref, o_ref):
    c = c_ref[...]
    s = c * jax.nn.sigmoid(c)
    hi = s.astype(BF16)
    lo = (s - hi.astype(F32)).astype(BF16)
    r = jnp.dot(jnp.concatenate([hi, lo], axis=0), w_ref[...].astype(BF16),
                preferred_element_type=F32)
    n = c.shape[0]
    o_ref[...] = (r[:n] + r[n:]) + b_ref[...]


def _modulation(c, w_ada, b_ada):
    rows = c.shape[0]
    n = w_ada.shape[1]
    bn = 1536
    return pl.pallas_call(
        _ada_kernel,
        grid=(n // bn,),
        in_specs=[pl.BlockSpec((rows, D_MODEL), lambda j: (0, 0)),
                  pl.BlockSpec((D_MODEL, bn), lambda j: (0, j)),
                  pl.BlockSpec((1, bn), lambda j: (0, j))],
        out_specs=pl.BlockSpec((rows, bn), lambda j: (0, j)),
        out_shape=jax.ShapeDtypeStruct((rows, n), F32),
        compiler_params=_params(),
        name="modulation",
    )(c, w_ada, b_ada.reshape(1, n))


def _norm_mod(x, g, scale, shift):
    ms = jnp.mean(x * x, axis=-1, keepdims=True)
    return ((x * lax.rsqrt(ms + EPS)) * g) * (1.0 + scale) + shift


def _inproj_kernel(x_ref, mod_ref, g_ref, w_ref, seg_ref, qg_ref, kg_ref,
                   qx_ref, k_ref, v_ref, u_ref):
    mod = mod_ref[0]
    h = _norm_mod(x_ref[...], g_ref[...], mod[1:2], mod[0:1]).astype(BF16)
    lane = lax.broadcasted_iota(jnp.int32, (TM_IN, LANES), 1)
    first_head = lane < HEAD_DIM

    def proj(c):
        return jnp.dot(h, w_ref[:, c * MXU_N:(c + 1) * MXU_N], preferred_element_type=F32)

    def head_norm(y, gain):
        ms = jnp.dot((y * y).astype(BF16), seg_ref[...], preferred_element_type=F32)
        return (y * lax.rsqrt(ms + EPS)) * gain

    n_chunks = ATT_WIDTH // MXU_N
    qk = [proj(c) for c in range(2 * n_chunks)]
    for j in range(2 * n_chunks):
        other = proj(2 * n_chunks + j).astype(BF16)
        if j < n_chunks:
            v_ref[:, j * MXU_N:(j + 1) * MXU_N] = other
            qn = head_norm(qk[j], qg_ref[:, j * MXU_N:(j + 1) * MXU_N])
            for pp in range(MXU_N // LANES):
                pair = qn[:, pp * LANES:(pp + 1) * LANES]
                base = (j * (MXU_N // LANES) + pp) * 2 * LANES
                qx_ref[:, base:base + LANES] = jnp.where(first_head, pair, 0.0).astype(BF16)
                qx_ref[:, base + LANES:base + 2 * LANES] = jnp.where(first_head, 0.0, pair).astype(BF16)
        else:
            c = j - n_chunks
            u_ref[:, c * MXU_N:(c + 1) * MXU_N] = other
            kn = head_norm(qk[j], kg_ref[:, c * MXU_N:(c + 1) * MXU_N])
            k_ref[:, c * MXU_N:(c + 1) * MXU_N] = kn.astype(BF16)


def _inproj(x2, mod3, boff, seq, g1, w_in, seg, qg, kg):
    n = x2.shape[0]
    tps = seq // TM_IN
    out_w = (2 * ATT_WIDTH, ATT_WIDTH, ATT_WIDTH, POOL_WIDTH)
    return pl.pallas_call(
        _inproj_kernel,
        grid=(n // TM_IN,),
        in_specs=[pl.BlockSpec((TM_IN, D_MODEL), lambda i: (i, 0)),
                  pl.BlockSpec((1, 6, D_MODEL), lambda i: (boff + i // tps, 0, 0)),
                  _const_spec((1, D_MODEL)),
                  _const_spec(w_in.shape),
                  _const_spec(seg.shape),
                  _const_spec((1, ATT_WIDTH)),
                  _const_spec((1, ATT_WIDTH))],
        out_specs=[pl.BlockSpec((TM_IN, w), lambda i: (i, 0)) for w in out_w],
        out_shape=[jax.ShapeDtypeStruct((n, w), BF16) for w in out_w],
        compiler_params=_params(),
        name="inproj",
    )(x2, mod3, g1, w_in, seg, qg, kg)


def _mix_kernel(q_ref, kp_ref, kc_ref, kn_ref, vp_ref, vc_ref, vn_ref,
                up_ref, uc_ref, un_ref, x_ref, mod_ref, bias_ref,
                wout_ref, o_ref,
                kbuf, vbuf, mixed, sbuf_a, sbuf_b, ubuf, lvl, *, tps, rows, seq):
    ti = lax.rem(pl.program_id(0), tps)
    r0 = ti * ROWS_PER_TILE

    kbuf[0:HALO_TOK, :] = kp_ref[...]
    kbuf[HALO_TOK:HALO_TOK + TM, :] = kc_ref[...]
    kbuf[HALO_TOK + TM:, :] = kn_ref[...]
    vbuf[0:HALO_TOK, :] = vp_ref[...]
    vbuf[HALO_TOK:HALO_TOK + TM, :] = vc_ref[...]
    vbuf[HALO_TOK + TM:, :] = vn_ref[...]

    lane = lax.broadcasted_iota(jnp.int32, (GRID_W, LANES), 1)
    n_keys = WIN_ROWS * GRID_W

    def window(lr):
        r = r0 + lr
        rs = jnp.clip(r - WIN_ROWS // 2, 0, rows - WIN_ROWS)
        off = pl.multiple_of((rs - r0 + HALO_ROWS) * GRID_W, GRID_W)
        return r - rs, off, pl.multiple_of(lr * GRID_W, GRID_W)

    def scores(win, dst, p):
        var, off, qoff = win
        qp = q_ref[pl.ds(qoff, GRID_W), p * 2 * LANES:(p + 1) * 2 * LANES]
        q2 = jnp.concatenate([qp[:, :LANES], qp[:, LANES:]], axis=0)
        kw = kbuf[pl.ds(off, n_keys), p * LANES:(p + 1) * LANES]
        s = lax.dot_general(q2, kw, (((1,), (1,)), ((), ())), preferred_element_type=F32)
        bias = jnp.concatenate([bias_ref[p, WIN_ROWS - 1 - var + 2 * j]
                                for j in range(WIN_ROWS // 2)], axis=-1)
        dst[p] = s + bias

    def attend(win, src, p):
        _, off, qoff = win
        s = src[p]
        m = jnp.max(s, axis=-1, keepdims=True)
        e = jnp.exp2(s - m)
        l = jnp.sum(e, axis=-1, keepdims=True)
        vw = vbuf[pl.ds(off, n_keys), p * LANES:(p + 1) * LANES]
        o2 = jnp.dot(e.astype(BF16), vw, preferred_element_type=F32) / l
        o = jnp.where(lane < HEAD_DIM, o2[:GRID_W], o2[GRID_W:])
        mixed[pl.ds(qoff, GRID_W), p * LANES:(p + 1) * LANES] = o.astype(BF16)

    def row_step(lr, src, dst):
        cur, nxt = window(lr), window(lr + 1)
        scores(nxt, dst, 0)
        scores(nxt, dst, 1)
        for p in range(N_PAIRS):
            attend(cur, src, p)
            if p + 2 < N_PAIRS:
                scores(nxt, dst, p + 2)

    for p in range(N_PAIRS):
        scores(window(0), sbuf_a, p)

    bufs = (sbuf_a, sbuf_b)

    def rows_block(j, carry):
        for a in range(ROW_UNROLL):
            row_step(ROW_UNROLL * j + a, bufs[a % 2], bufs[(a + 1) % 2])
        return carry

    n_blocks = ROWS_PER_TILE // ROW_UNROLL - 1
    lax.fori_loop(0, n_blocks, rows_block, 0)
    for lr in range(n_blocks * ROW_UNROLL, ROWS_PER_TILE - 1):
        row_step(lr, bufs[lr % 2], bufs[(lr + 1) % 2])
    for p in range(N_PAIRS):
        attend(window(ROWS_PER_TILE - 1), bufs[(ROWS_PER_TILE - 1) % 2], p)

    ext = TM + 4 * SUB
    inner = TM + 2 * SUB
    centre = 2 * SUB
    ubuf[0:SUB, :] = jnp.zeros((SUB, POOL_WIDTH), F32)
    ubuf[ext - SUB:ext, :] = jnp.zeros((SUB, POOL_WIDTH), F32)
    ubuf[SUB:centre, :] = jnp.where(ti > 0, up_ref[...].astype(F32)[SUB:], 0.0)
    ubuf[centre:centre + TM, :] = uc_ref[...].astype(F32)
    ubuf[centre + TM:ext - SUB, :] = jnp.where(ti < tps - 1, un_ref[...].astype(F32)[:SUB], 0.0)
    lvl[:, :, 0:SUB, :] = jnp.zeros(lvl.shape[:2] + (SUB, POOL_GROUP), F32)
    lvl[:, :, ext - SUB:ext, :] = jnp.zeros(lvl.shape[:2] + (SUB, POOL_GROUP), F32)
    tpos = ti * TM + lax.broadcasted_iota(jnp.int32, (TM, POOL_GROUP), 0)
    for g, w in enumerate(POOL_WINDOWS):
        cols = slice(g * POOL_GROUP, (g + 1) * POOL_GROUP)
        if w == 2:
            acc = ubuf[pl.ds(centre - 1, TM), cols] + ubuf[pl.ds(centre, TM), cols]
        else:
            lv = lvl.at[g - 1]
            lv[0, SUB:SUB + inner, :] = (ubuf[pl.ds(SUB - 1, inner), cols]
                                         + ubuf[pl.ds(SUB, inner), cols])
            half, k = 1, 0
            while 4 * half < w:
                lv[1 - k, SUB:SUB + inner, :] = (lv[k, pl.ds(SUB - half, inner), :]
                                                 + lv[k, pl.ds(SUB + half, inner), :])
                half, k = 2 * half, 1 - k
            acc = lv[k, pl.ds(centre - half, TM), :] + lv[k, pl.ds(centre + half, TM), :]
        lo = jnp.maximum(tpos - w // 2, 0)
        hi = jnp.minimum(tpos - w // 2 + w, seq)
        cnt = (hi - lo).astype(F32)
        pooled = (acc / cnt - ubuf[centre:centre + TM, cols]).astype(BF16)
        mixed[:, ATT_WIDTH + g * POOL_GROUP:ATT_WIDTH + (g + 1) * POOL_GROUP] = pooled

    out = jnp.dot(mixed[...], wout_ref[...], preferred_element_type=F32)
    o_ref[...] = x_ref[...] + mod_ref[0][2:3] * out


def _fold_kernel(wp_ref, ps_ref, wo_ref, o_ref):
    w = wp_ref[0] * ps_ref[0]
    o_ref[...] = jnp.dot(w, wo_ref[...], preferred_element_type=F32,
                         precision=lax.Precision.HIGHEST)


def _fold_pool(w_pool, pool_scale, w_out):
    n_g = len(POOL_WINDOWS)
    first = ATT_WIDTH // POOL_GROUP
    return pl.pallas_call(
        _fold_kernel,
        grid=(n_g,),
        in_specs=[pl.BlockSpec((1, POOL_GROUP, POOL_GROUP), lambda g: (g, 0, 0)),
                  pl.BlockSpec((1, 1, POOL_GROUP), lambda g: (g, 0, 0)),
                  pl.BlockSpec((POOL_GROUP, D_MODEL), lambda g: (first + g, 0))],
        out_specs=pl.BlockSpec((POOL_GROUP, D_MODEL), lambda g: (g, 0)),
        out_shape=jax.ShapeDtypeStruct((POOL_WIDTH, D_MODEL), F32),
        compiler_params=_params(),
        name="fold_pool",
    )(w_pool, pool_scale.reshape(n_g, 1, POOL_GROUP), w_out)


def _mix(x2, qx, k, v, u, mod3, boff, seq, bias, w_out):
    n = x2.shape[0]
    tps = seq // TM
    rows = seq // GRID_W
    hb = TM // HALO_TOK
    n_hb = n // HALO_TOK
    ub = TM // SUB16
    n_ub = n // SUB16
    cur = lambda i: (i, 0)
    kprev = lambda i: (jnp.maximum(i * hb - 1, 0), 0)
    knext = lambda i: (jnp.minimum((i + 1) * hb, n_hb - 1), 0)
    uprev = lambda i: (jnp.maximum(i * ub - 1, 0), 0)
    unext = lambda i: (jnp.minimum((i + 1) * ub, n_ub - 1), 0)
    kernel = functools.partial(_mix_kernel, tps=tps, rows=rows, seq=seq)
    return pl.pallas_call(
        kernel,
        grid=(n // TM,),
        in_specs=[pl.BlockSpec((TM, 2 * ATT_WIDTH), cur),
                  pl.BlockSpec((HALO_TOK, ATT_WIDTH), kprev),
                  pl.BlockSpec((TM, ATT_WIDTH), cur),
                  pl.BlockSpec((HALO_TOK, ATT_WIDTH), knext),
                  pl.BlockSpec((HALO_TOK, ATT_WIDTH), kprev),
                  pl.BlockSpec((TM, ATT_WIDTH), cur),
                  pl.BlockSpec((HALO_TOK, ATT_WIDTH), knext),
                  pl.BlockSpec((SUB16, POOL_WIDTH), uprev),
                  pl.BlockSpec((TM, POOL_WIDTH), cur),
                  pl.BlockSpec((SUB16, POOL_WIDTH), unext),
                  pl.BlockSpec((TM, D_MODEL), cur),
                  pl.BlockSpec((1, 6, D_MODEL), lambda i: (boff + i // tps, 0, 0)),
                  _const_spec(bias.shape),
                  _const_spec(w_out.shape)],
        out_specs=pl.BlockSpec((TM, D_MODEL), cur),
        out_shape=jax.ShapeDtypeStruct((n, D_MODEL), F32),
        scratch_shapes=[pltpu.VMEM((TM + 2 * HALO_TOK, ATT_WIDTH), BF16),
                        pltpu.VMEM((TM + 2 * HALO_TOK, ATT_WIDTH), BF16),
                        pltpu.VMEM((TM, D_MODEL), BF16),
                        pltpu.VMEM((N_PAIRS, 2 * GRID_W, WIN_ROWS * GRID_W), F32),
                        pltpu.VMEM((N_PAIRS, 2 * GRID_W, WIN_ROWS * GRID_W), F32),
                        pltpu.VMEM((TM + 4 * SUB, POOL_WIDTH), F32),
                        pltpu.VMEM((len(POOL_WINDOWS) - 1, 2, TM + 4 * SUB, POOL_GROUP), F32)],
        compiler_params=_params(),
        name="mix",
    )(qx, k, k, k, v, v, v, u, u, u, x2, mod3, bias, w_out)


def _mlp_kernel(xp_ref, xc_ref, xn_ref, mod_ref, g_ref, wup_ref, cw_ref, cb_ref, wdown_ref,
                o_ref, act, *, tps, tm):
    ti = lax.rem(pl.program_id(0), tps)
    mod = mod_ref[0]
    shift, scale, gate = mod[3:4], mod[4:5], mod[5:6]
    g = g_ref[...]
    xc = xc_ref[...]
    hp = jnp.where(ti > 0, _norm_mod(xp_ref[...], g, scale, shift), 0.0)
    hn = jnp.where(ti < tps - 1, _norm_mod(xn_ref[...], g, scale, shift), 0.0)
    h = jnp.concatenate([hp, _norm_mod(xc, g, scale, shift), hn], axis=0).astype(BF16)
    ext = tm + 2 * SUB

    def conv(cols):
        up = jnp.dot(h, wup_ref[:, cols], preferred_element_type=F32)
        cw = cw_ref[:, cols]
        prev = pltpu.roll(up, 1, 0)[SUB:SUB + tm].astype(BF16)
        nxt = pltpu.roll(up, ext - 1, 0)[SUB:SUB + tm].astype(BF16)
        cur = up[SUB:SUB + tm].astype(BF16)
        return (prev * cw[0:1].astype(BF16) + cur * cw[1:2].astype(BF16)
                + nxt * cw[2:3].astype(BF16) + cb_ref[:, cols].astype(BF16))

    for lo in range(0, D_FF, FC):
        hi = min(lo + FC, D_FF)
        gb = conv(slice(lo, hi))
        vb = conv(slice(D_FF + lo, D_FF + hi))
        act[:, lo:hi] = (gb * jax.nn.sigmoid(gb)) * vb

    y = jnp.dot(act[...], wdown_ref[...], preferred_element_type=F32)
    o_ref[...] = xc + gate * y


def _mlp(x1, mod3, boff, seq, g2, w_up, conv_w, conv_b, w_down):
    n = x1.shape[0]
    tm = TM_MLP
    tps = seq // tm
    hb = tm // SUB
    n_hb = n // SUB
    kernel = functools.partial(_mlp_kernel, tps=tps, tm=tm)
    return pl.pallas_call(
        kernel,
        grid=(n // tm,),
        in_specs=[pl.BlockSpec((SUB, D_MODEL), lambda i: (jnp.maximum(i * hb - 1, 0), 0)),
                  pl.BlockSpec((tm, D_MODEL), lambda i: (i, 0)),
                  pl.BlockSpec((SUB, D_MODEL), lambda i: (jnp.minimum((i + 1) * hb, n_hb - 1), 0)),
                  pl.BlockSpec((1, 6, D_MODEL), lambda i: (boff + i // tps, 0, 0)),
                  _const_spec((1, D_MODEL)),
                  _const_spec(w_up.shape),
                  _const_spec(conv_w.shape),
                  _const_spec((1, 2 * D_FF)),
                  _const_spec(w_down.shape)],
        out_specs=pl.BlockSpec((tm, D_MODEL), lambda i: (i, 0)),
        out_shape=jax.ShapeDtypeStruct((n, D_MODEL), F32),
        scratch_shapes=[pltpu.VMEM((tm, D_FF), BF16)],
        compiler_params=_params(),
        name="mlp",
    )(x1, x1, x1, mod3, g2, w_up, conv_w, conv_b, w_down)


def _bias_table(rpb):
    cols = np.arange(GRID_W)
    start = np.clip(cols - WIN_COLS // 2, 0, GRID_W - WIN_COLS)
    onehot = np.zeros((2 * WIN_COLS - 1, GRID_W, GRID_W), np.float32)
    inside = np.zeros((GRID_W, GRID_W), bool)
    for c in cols:
        for j in range(WIN_COLS):
            c2 = start[c] + j
            onehot[c2 - c + WIN_COLS - 1, c, c2] = 1.0
            inside[c, c2] = True
    t = jnp.einsum('hrd,dcx->hrcx', rpb.astype(F32), jnp.asarray(onehot),
                   precision=lax.Precision.HIGHEST)
    t = jnp.where(jnp.asarray(inside), t * LOG2E, NEG)
    two = jnp.concatenate([t[:, :-1], t[:, 1:]], axis=-1)
    two = two.reshape(N_PAIRS, 2, 2 * WIN_ROWS - 2, GRID_W, 2 * GRID_W).transpose(0, 2, 1, 3, 4)
    return two.reshape(N_PAIRS, 2 * WIN_ROWS - 2, 2 * GRID_W, 2 * GRID_W)


def _layer(xs, mod3, p):
    (norm1_g, norm2_g, w_in, q_norm_g, k_norm_g, rpb, w_pool, pool_scale, w_out,
     w_up, conv_w, conv_b, w_down) = p
    g1 = norm1_g.reshape(1, D_MODEL)
    g2 = norm2_g.reshape(1, D_MODEL)
    w_in_b = w_in.astype(BF16)
    heads = np.arange(MXU_N) // HEAD_DIM
    seg = jnp.asarray((heads[:, None] == heads[None, :]).astype(np.float32) / HEAD_DIM, BF16)
    qg = (jnp.tile(q_norm_g, N_HEADS) * (HEAD_DIM ** -0.5 * LOG2E)).reshape(1, ATT_WIDTH)
    kg = jnp.tile(k_norm_g, N_HEADS).reshape(1, ATT_WIDTH)
    bias = _bias_table(rpb)
    w_out_b = jnp.concatenate([w_out[:ATT_WIDTH], _fold_pool(w_pool, pool_scale, w_out)],
                              axis=0).astype(BF16)
    w_up_b = w_up.astype(BF16)
    cb = conv_b.reshape(1, 2 * D_FF)
    w_down_b = w_down.astype(BF16)

    outs = []
    boff = 0
    for x in xs:
        b, seq, _ = x.shape
        x2 = x.reshape(b * seq, D_MODEL)
        qx, k, v, u = _inproj(x2, mod3, boff, seq, g1, w_in_b, seg, qg, kg)
        x1 = _mix(x2, qx, k, v, u, mod3, boff, seq, bias, w_out_b)
        y = _mlp(x1, mod3, boff, seq, g2, w_up_b, conv_w, cb, w_down_b)
        outs.append(y.reshape(b, seq, D_MODEL))
        boff += b
    return outs


def kernel(x_prompt, x_sample, c_prompt, c_sample, w_ada, b_ada, norm1_g, norm2_g, w_in,
           q_norm_g, k_norm_g, rpb, w_pool, pool_scale, w_out, w_up, conv_w, conv_b, w_down):
    xs = [x_prompt, x_sample]
    c = jnp.concatenate([c_prompt, c_sample], axis=0)
    n_b = c.shape[0]
    c = jnp.pad(c, ((0, -n_b % SUB16), (0, 0)))
    for l in range(w_ada.shape[0]):
        mod = _modulation(c, w_ada[l], b_ada[l])
        mod3 = mod.reshape(c.shape[0], 6, D_MODEL)
        p = (norm1_g[l], norm2_g[l], w_in[l], q_norm_g[l], k_norm_g[l], rpb[l], w_pool[l],
             pool_scale[l], w_out[l], w_up[l], conv_w[l], conv_b[l], w_down[l])
        xs = _layer(xs, mod3, p)
    return (xs[0], xs[1])
```
